```python
import math
import jax, jax.numpy as jnp
from jax import lax
import numpy as np

D_MODEL = 1024
BATCH = 4
SEQ = 8192
DEPTH = 2

CHUNK = 64
HEAD_DIM = 64
N_HEADS_A = 8
N_HEADS_B = 8
WIDTH_A = N_HEADS_A * HEAD_DIM
WIDTH_B = N_HEADS_B * HEAD_DIM
LEFT_CHUNKS = 8
BAND = (LEFT_CHUNKS + 1) * CHUNK
REL_CLIP = 256
N_REL = 2 * REL_CLIP + 1
SB_BLOCK = 128
D_FF = -(-8 * D_MODEL // (3 * 256)) * 256
IN_COLS = 3 * WIDTH_A + 3 * WIDTH_B + 2 * D_MODEL
DEEPNORM_ALPHA = (2 * DEPTH) ** 0.25
DEEPNORM_BETA = (8 * DEPTH) ** -0.25
LN_EPS = 1e-5

kernel_name = "hybrid_chunk_relbias_stickbreaking_deepnorm"


def layer_norm(x, g, b):
    xf = x.astype(jnp.float32)
    mu = jnp.mean(xf, axis=-1, keepdims=True)
    var = jnp.mean(jnp.square(xf - mu), axis=-1, keepdims=True)
    y = (xf - mu) * lax.rsqrt(var + LN_EPS) * g.astype(jnp.float32) + b.astype(jnp.float32)
    return y.astype(x.dtype)


def chunk_band_attention(q, k, v, rel_bias):
    B, S, H, dh = q.shape
    nc = S // CHUNK
    qc = q.reshape(B, nc, CHUNK, H, dh)
    pad = ((0, 0), (LEFT_CHUNKS * CHUNK, 0), (0, 0), (0, 0))
    kp = jnp.pad(k, pad).reshape(B, nc + LEFT_CHUNKS, CHUNK, H, dh)
    vp = jnp.pad(v, pad).reshape(B, nc + LEFT_CHUNKS, CHUNK, H, dh)
    kb = jnp.concatenate([kp[:, j:j + nc] for j in range(LEFT_CHUNKS + 1)], axis=2)
    vb = jnp.concatenate([vp[:, j:j + nc] for j in range(LEFT_CHUNKS + 1)], axis=2)
    scores = jnp.einsum('bcqhd,bckhd->bhcqk', qc, kb).astype(jnp.float32) / math.sqrt(dh)
    i = jnp.arange(CHUNK)[:, None]
    p = jnp.arange(BAND)[None, :]
    dist = LEFT_CHUNKS * CHUNK + i - p
    idx = jnp.clip(dist, -REL_CLIP, REL_CLIP) + REL_CLIP
    bias = rel_bias.astype(jnp.float32)[:, idx]
    valid = (jnp.arange(nc)[:, None] + jnp.arange(BAND)[None, :] // CHUNK - LEFT_CHUNKS) >= 0
    scores = scores + bias[None, :, None, :, :]
    scores = jnp.where(valid[None, None, :, None, :], scores, -jnp.inf)
    probs = jax.nn.softmax(scores, axis=-1).astype(v.dtype)
    out = jnp.einsum('bhcqk,bckhd->bcqhd', probs, vb)
    return out.reshape(B, S, H * dh)


def stick_breaking_attention(q, k, v):
    B, S, H, dh = q.shape
    nb = S // SB_BLOCK
    scale = 1.0 / math.sqrt(dh)
    qb = q.reshape(B, nb, SB_BLOCK, H, dh).transpose(1, 0, 2, 3, 4)
    key_pos = jnp.arange(S)

    def one_block(args):
        q_blk, blk = args
        z = jnp.einsum('bqhd,bshd->bhqs', q_blk, k).astype(jnp.float32) * scale
        t = blk * SB_BLOCK + jnp.arange(SB_BLOCK)
        causal = (key_pos[None, :] < t[:, None])[None, None]
        log_keep = jnp.where(causal, jax.nn.log_sigmoid(-z), 0.0)
        suffix = lax.cumsum(log_keep, axis=3, reverse=True) - log_keep
        log_w = jnp.where(causal, jax.nn.log_sigmoid(z) + suffix, -jnp.inf)
        w = jnp.exp(log_w).astype(v.dtype)
        return jnp.einsum('bhqs,bshd->bqhd', w, v)

    out = lax.map(one_block, (qb, jnp.arange(nb)))
    return out.transpose(1, 0, 2, 3, 4).reshape(B, S, H * dh)


def setup_inputs(seed: int = 0) -> dict:
    key = jax.random.key(seed)
    ks = jax.random.split(key, 16)
    D = D_MODEL
    x = jax.random.normal(ks[0], (BATCH, SEQ, D), jnp.float32)
    w_in = jax.random.normal(ks[1], (DEPTH, D, IN_COLS), jnp.float32) * D ** -0.5
    col_scale = jnp.concatenate([
        jnp.ones((2 * WIDTH_A,)), jnp.full((WIDTH_A,), DEEPNORM_BETA),
        jnp.ones((2 * WIDTH_B,)), jnp.full((WIDTH_B,), DEEPNORM_BETA),
        jnp.ones((2 * D,))]).astype(jnp.float32)
    w_in = w_in * col_scale
    b_gate = 0.01 * jax.random.normal(ks[2], (DEPTH, 2 * D), jnp.float32)
    rel_bias = 0.1 * jax.random.normal(ks[3], (DEPTH, N_HEADS_A, N_REL), jnp.float32)
    w_proj_a = jax.random.normal(ks[4], (DEPTH, WIDTH_A, D), jnp.float32) * WIDTH_A ** -0.5
    w_proj_b = jax.random.normal(ks[5], (DEPTH, WIDTH_B, D), jnp.float32) * WIDTH_B ** -0.5
    w_out = jax.random.normal(ks[6], (DEPTH, D, D), jnp.float32) * (D ** -0.5 * DEEPNORM_BETA)
    ln1_g = 1.0 + 0.02 * jax.random.normal(ks[7], (DEPTH, D), jnp.float32)
    ln1_b = 0.02 * jax.random.normal(ks[8], (DEPTH, D), jnp.float32)
    w_ffn_in = jax.random.normal(ks[9], (DEPTH, D, 2 * D_FF), jnp.float32) * D ** -0.5
    w_ffn_out = jax.random.normal(ks[10], (DEPTH, D_FF, D), jnp.float32) * (D_FF ** -0.5 * DEEPNORM_BETA)
    ln2_g = 1.0 + 0.02 * jax.random.normal(ks[11], (DEPTH, D), jnp.float32)
    ln2_b = 0.02 * jax.random.normal(ks[12], (DEPTH, D), jnp.float32)
    return {"x": x, "w_in": w_in, "b_gate": b_gate, "rel_bias": rel_bias,
            "w_proj_a": w_proj_a, "w_proj_b": w_proj_b, "w_out": w_out,
            "ln1_g": ln1_g, "ln1_b": ln1_b, "w_ffn_in": w_ffn_in,
            "w_ffn_out": w_ffn_out, "ln2_g": ln2_g, "ln2_b": ln2_b}


def reference(x, w_in, b_gate, rel_bias, w_proj_a, w_proj_b, w_out,
              ln1_g, ln1_b, w_ffn_in, w_ffn_out, ln2_g, ln2_b):
    B, S, D = x.shape
    split_pts = np.cumsum([WIDTH_A, WIDTH_A, WIDTH_A, WIDTH_B, WIDTH_B, WIDTH_B, D]).tolist()
    for l in range(DEPTH):
        h = x @ w_in[l]
        qa, ka, va, qb, kb, vb, ga, gb = jnp.split(h, split_pts, axis=-1)
        heads_a = lambda t: t.reshape(B, S, N_HEADS_A, HEAD_DIM)
        heads_b = lambda t: t.reshape(B, S, N_HEADS_B, HEAD_DIM)
        y_a = chunk_band_attention(heads_a(qa), heads_a(ka), heads_a(va), rel_bias[l]) @ w_proj_a[l]
        y_b = stick_breaking_attention(heads_b(qb), heads_b(kb), heads_b(vb)) @ w_proj_b[l]
        gate_a = jax.nn.sigmoid(ga + b_gate[l, :D])
        gate_b = jax.nn.sigmoid(gb + b_gate[l, D:])
        mix = (gate_a * y_a + gate_b * y_b) @ w_out[l]
        x = layer_norm(DEEPNORM_ALPHA * x + mix, ln1_g[l], ln1_b[l])
        gu = x @ w_ffn_in[l]
        g, u = jnp.split(gu, 2, axis=-1)
        ffn = (jax.nn.silu(g) * u) @ w_ffn_out[l]
        x = layer_norm(DEEPNORM_ALPHA * x + ffn, ln2_g[l], ln2_b[l])
    return x
```

```python
import functools
import math

import jax
import jax.numpy as jnp
from jax import lax
from jax.experimental import pallas as pl
from jax.experimental.pallas import tpu as pltpu

D_MODEL = 1024
HEAD_DIM = 64
WIDTH_A = 512
WIDTH_B = 512
CHUNK = 64
LEFT_CHUNKS = 8
REL_CLIP = 256
D_FF = 2816
DEPTH = 2
DEEPNORM_ALPHA = (2 * DEPTH) ** 0.25
LN_EPS = 1e-5
LOG2E = 1.4426950408889634
Q_SCALE = LOG2E / math.sqrt(HEAD_DIM)
NEG_BIG = -1e30

LANES = 128
HEADS_PER_BLOCK = LANES // HEAD_DIM
VMEM_LIMIT = 56 * 1024 * 1024

TM_PROJ = 512
TM_MERGE = 512
TM_FFN = 256
TQ_A = 256
KB_A = 128
T_B = 256

BF16 = jnp.bfloat16
F32 = jnp.float32


def _dot(a, b):
    return jnp.dot(a, b, preferred_element_type=F32)


def _dot_nt(a, b):
    return lax.dot_general(a, b, (((1,), (1,)), ((), ())), preferred_element_type=F32)


def _layer_norm(r, g, b):
    mu = jnp.mean(r, axis=-1, keepdims=True)
    d = r - mu
    var = jnp.mean(d * d, axis=-1, keepdims=True)
    return d * lax.rsqrt(var + LN_EPS) * g + b


def _sigmoid(v):
    return 1.0 / (1.0 + jnp.exp(-v))


def _inproj_kernel(x_ref, w_ref, bg_ref, qa_ref, ka_ref, va_ref, qb_ref, kb_ref, vb_ref,
                   ga_ref, gb_ref):
    xb = x_ref[...].astype(BF16)

    def proj(c0, width):
        return _dot(xb, w_ref[:, c0:c0 + width])

    c = 0
    qa_ref[...] = (proj(c, WIDTH_A) * Q_SCALE).astype(BF16)
    c += WIDTH_A
    ka_ref[...] = proj(c, WIDTH_A).astype(BF16)
    c += WIDTH_A
    va_ref[...] = proj(c, WIDTH_A).astype(BF16)
    c += WIDTH_A
    qb_ref[...] = (proj(c, WIDTH_B) * Q_SCALE).astype(BF16)
    c += WIDTH_B
    kb_ref[...] = proj(c, WIDTH_B).astype(BF16)
    c += WIDTH_B
    vb_ref[...] = proj(c, WIDTH_B).astype(BF16)
    c += WIDTH_B
    ga_ref[...] = _sigmoid(proj(c, D_MODEL) + bg_ref[:, :D_MODEL]).astype(BF16)
    c += D_MODEL
    gb_ref[...] = _sigmoid(proj(c, D_MODEL) + bg_ref[:, D_MODEL:]).astype(BF16)


def _inproj(x, w, bg):
    n = x.shape[0]
    cols = w.shape[1]
    widths = [WIDTH_A] * 3 + [WIDTH_B] * 3 + [D_MODEL] * 2
    return pl.pallas_call(
        _inproj_kernel,
        grid=(n // TM_PROJ,),
        in_specs=[
            pl.BlockSpec((TM_PROJ, D_MODEL), lambda i: (i, 0)),
            pl.BlockSpec((D_MODEL, cols), lambda i: (0, 0)),
            pl.BlockSpec((1, 2 * D_MODEL), lambda i: (0, 0)),
        ],
        out_specs=[pl.BlockSpec((TM_PROJ, wd), lambda i: (i, 0)) for wd in widths],
        out_shape=[jax.ShapeDtypeStruct((n, wd), BF16) for wd in widths],
        compiler_params=pltpu.CompilerParams(
            dimension_semantics=("parallel",), vmem_limit_bytes=VMEM_LIMIT),
        name="inproj",
    )(x, w, bg)


def _mixer_a_kernel(q_ref, k_ref, v_ref, eb_ref, o_ref, bias_ref, *, tq, seq):
    left = LEFT_CHUNKS * CHUNK
    win = left + tq
    nkb = win // KB_A
    hp = pl.program_id(1)
    qi = pl.program_id(2)
    t0 = qi * tq

    @pl.when(qi == 0)
    def _build_bias():
        qc = lax.broadcasted_iota(jnp.int32, (tq, win), 0) // CHUNK
        kc = lax.broadcasted_iota(jnp.int32, (tq, win), 1) // CHUNK
        band = (kc >= qc) & (kc <= qc + LEFT_CHUNKS)
        for h in range(HEADS_PER_BLOCK):
            e = eb_ref[pl.ds(hp * HEADS_PER_BLOCK + h, 1), :]
            rows = jnp.broadcast_to(e, (tq, e.shape[1]))
            toep = pltpu.roll(rows, 0, 1, stride=1, stride_axis=0)[:, :win]
            bias_ref[h] = jnp.where(band, toep * LOG2E, NEG_BIG)

    q = q_ref[0]
    lane = lax.broadcasted_iota(jnp.int32, q.shape, 1)
    outs = []
    for h in range(HEADS_PER_BLOCK):
        in_head = (lane >= h * HEAD_DIM) & (lane < (h + 1) * HEAD_DIM)
        qm = jnp.where(in_head, q, jnp.zeros_like(q))
        s_blocks = []
        v_blocks = []
        for kb in range(nkb):
            start = t0 - left + kb * KB_A
            ok = start >= 0
            st = pl.multiple_of(jnp.maximum(start, 0), KB_A)
            kblk = k_ref[0, pl.ds(st, KB_A), :]
            v_blocks.append(v_ref[0, pl.ds(st, KB_A), :])
            s = _dot_nt(qm, kblk) + bias_ref[h, :, kb * KB_A:(kb + 1) * KB_A]
            s_blocks.append(jnp.where(ok, s, NEG_BIG))
        m = s_blocks[0].max(axis=-1, keepdims=True)
        for s in s_blocks[1:]:
            m = jnp.maximum(m, s.max(axis=-1, keepdims=True))
        l = jnp.zeros((tq, 1), F32)
        acc = jnp.zeros((tq, LANES), F32)
        for s, vblk in zip(s_blocks, v_blocks):
            p = jnp.exp2(s - m)
            l = l + p.sum(axis=-1, keepdims=True)
            acc = acc + _dot(p.astype(BF16), vblk)
        outs.append(acc / l)
    out = outs[0]
    for h in range(1, HEADS_PER_BLOCK):
        out = jnp.where(lane >= h * HEAD_DIM, outs[h], out)
    o_ref[0] = out.astype(o_ref.dtype)


def _mixer_a(q, k, v, ebias):
    b, seq, width = q.shape
    tq = TQ_A
    win = LEFT_CHUNKS * CHUNK + tq
    kern = functools.partial(_mixer_a_kernel, tq=tq, seq=seq)
    return pl.pallas_call(
        kern,
        grid=(b, width // LANES, seq // tq),
        in_specs=[
            pl.BlockSpec((1, tq, LANES), lambda bi, hp, qi: (bi, qi, hp)),
            pl.BlockSpec((1, seq, LANES), lambda bi, hp, qi: (bi, 0, hp)),
            pl.BlockSpec((1, seq, LANES), lambda bi, hp, qi: (bi, 0, hp)),
            pl.BlockSpec(ebias.shape, lambda bi, hp, qi: (0, 0)),
        ],
        out_specs=pl.BlockSpec((1, tq, LANES), lambda bi, hp, qi: (bi, qi, hp)),
        out_shape=jax.ShapeDtypeStruct((b, seq, width), BF16),
        scratch_shapes=[pltpu.VMEM((HEADS_PER_BLOCK, tq, win), F32)],
        compiler_params=pltpu.CompilerParams(
            dimension_semantics=("parallel", "parallel", "arbitrary"),
            vmem_limit_bytes=VMEM_LIMIT),
        name="mixer_a",
    )(q, k, v, ebias)


def _extended_bias(rel_bias, tq):
    left = LEFT_CHUNKS * CHUNK
    top = rel_bias[:, 2 * REL_CLIP:]
    n_flat = left - REL_CLIP + 1
    n_rev = left + tq - n_flat
    lo = 2 * REL_CLIP - n_rev
    rev = rel_bias[:, lo:2 * REL_CLIP][:, ::-1]
    h = rel_bias.shape[0]
    return jnp.concatenate(
        [jnp.broadcast_to(top, (h, n_flat)), rev, jnp.broadcast_to(top, (h, tq))], axis=1)


def _softplus2(z):
    return jnp.maximum(z, 0.0) + jnp.log2(1.0 + jnp.exp2(-jnp.abs(z)))


def _split_hi_lo(x):
    hi = x.astype(BF16)
    lo = (x - hi.astype(F32)).astype(BF16)
    return jnp.concatenate([hi, lo], axis=1)


def _mixer_b_kernel(q_ref, k_ref, v_ref, negl_ref, o_ref, acc_ref, carry_ref, *, t):
    qi = pl.program_id(2)
    t0 = pl.multiple_of(qi * t, t)
    q = q_ref[0]
    lane = lax.broadcasted_iota(jnp.int32, q.shape, 1)
    qms = []
    for h in range(HEADS_PER_BLOCK):
        in_head = (lane >= h * HEAD_DIM) & (lane < (h + 1) * HEAD_DIM)
        qms.append(jnp.where(in_head, q, jnp.zeros_like(q)))
    negl = negl_ref[...]
    reps = t // LANES

    row = lax.broadcasted_iota(jnp.int32, (t, t), 0)
    col = lax.broadcasted_iota(jnp.int32, (t, t), 1)
    causal = col < row
    kd = k_ref[0, pl.ds(t0, t), :]
    vd = v_ref[0, pl.ds(t0, t), :]
    for h in range(HEADS_PER_BLOCK):
        z = _dot_nt(qms[h], kd)
        sp = jnp.where(causal, _softplus2(z), 0.0)
        negc = _dot(_split_hi_lo(sp), negl)
        w = jnp.where(causal, jnp.exp2(z + negc), 0.0)
        acc_ref[h] = _dot(w.astype(BF16), vd)
        carry_ref[h] = jnp.broadcast_to(negc[:, 0:1], (t, LANES))

    def body(it, _):
        k0 = pl.multiple_of((qi - 1 - it) * t, t)
        kt = k_ref[0, pl.ds(k0, t), :]
        vt = v_ref[0, pl.ds(k0, t), :]
        for h in range(HEADS_PER_BLOCK):
            z = _dot_nt(qms[h], kt)
            negc = _dot(_split_hi_lo(_softplus2(z)), negl)
            carry = carry_ref[h]
            tot = negc + jnp.concatenate([carry] * reps, axis=1)
            w = jnp.exp2(z + tot)
            acc_ref[h] += _dot(w.astype(BF16), vt)
            carry_ref[h] = jnp.broadcast_to(tot[:, 0:1], (t, LANES))
        return 0

    lax.fori_loop(0, qi, body, 0)

    out = acc_ref[0]
    for h in range(1, HEADS_PER_BLOCK):
        out = jnp.where(lane >= h * HEAD_DIM, acc_ref[h], out)
    o_ref[0] = out.astype(o_ref.dtype)


def _mixer_b(q, k, v):
    b, seq, width = q.shape
    t = T_B
    idx = jnp.arange(t)
    tri = jnp.where(idx[:, None] >= idx[None, :], -1.0, 0.0).astype(BF16)
    negl = jnp.concatenate([tri, tri], axis=0)
    kern = functools.partial(_mixer_b_kernel, t=t)
    return pl.pallas_call(
        kern,
        grid=(b, width // LANES, seq // t),
        in_specs=[
            pl.BlockSpec((1, t, LANES), lambda bi, hp, qi: (bi, qi, hp)),
            pl.BlockSpec((1, seq, LANES), lambda bi, hp, qi: (bi, 0, hp)),
            pl.BlockSpec((1, seq, LANES), lambda bi, hp, qi: (bi, 0, hp)),
            pl.BlockSpec((2 * t, t), lambda bi, hp, qi: (0, 0)),
        ],
        out_specs=pl.BlockSpec((1, t, LANES), lambda bi, hp, qi: (bi, qi, hp)),
        out_shape=jax.ShapeDtypeStruct((b, seq, width), BF16),
        scratch_shapes=[pltpu.VMEM((HEADS_PER_BLOCK, t, LANES), F32),
                        pltpu.VMEM((HEADS_PER_BLOCK, t, LANES), F32)],
        compiler_params=pltpu.CompilerParams(
            dimension_semantics=("parallel", "parallel", "arbitrary"),
            vmem_limit_bytes=VMEM_LIMIT),
        name="mixer_b",
    )(q, k, v, negl)


def _merge_kernel(x_ref, a_ref, b_ref, ga_ref, gb_ref, wa_ref, wb_ref, wo_ref, g_ref, beta_ref,
                  o_ref):
    ya = _dot(a_ref[...], wa_ref[...])
    yb = _dot(b_ref[...], wb_ref[...])
    m = ga_ref[...].astype(F32) * ya + gb_ref[...].astype(F32) * yb
    mix = _dot(m.astype(BF16), wo_ref[...])
    r = DEEPNORM_ALPHA * x_ref[...] + mix
    o_ref[...] = _layer_norm(r, g_ref[...], beta_ref[...])


def _merge(x, a, bm, ga, gb, wa, wb, wo, g, beta):
    n = x.shape[0]
    tm = TM_MERGE
    row = lambda wd: pl.BlockSpec((tm, wd), lambda i: (i, 0))
    full = lambda arr: pl.BlockSpec(arr.shape, lambda i: (0, 0))
    return pl.pallas_call(
        _merge_kernel,
        grid=(n // tm,),
        in_specs=[row(D_MODEL), row(WIDTH_A), row(WIDTH_B), row(D_MODEL), row(D_MODEL),
                  full(wa), full(wb), full(wo), full(g), full(beta)],
        out_specs=row(D_MODEL),
        out_shape=jax.ShapeDtypeStruct((n, D_MODEL), F32),
        compiler_params=pltpu.CompilerParams(
            dimension_semantics=("parallel",), vmem_limit_bytes=VMEM_LIMIT),
        name="merge_ln",
    )(x, a, bm, ga, gb, wa, wb, wo, g, beta)


def _ffn_kernel(x_ref, w1_ref, w2_ref, g_ref, beta_ref, o_ref):
    x = x_ref[...]
    xb = x.astype(BF16)
    gate = _dot(xb, w1_ref[:, :D_FF])
    up = _dot(xb, w1_ref[:, D_FF:])
    act = gate * _sigmoid(gate) * up
    ffn = _dot(act.astype(BF16), w2_ref[...])
    r = DEEPNORM_ALPHA * x + ffn
    o_ref[...] = _layer_norm(r, g_ref[...], beta_ref[...])


def _ffn(x, w1, w2, g, beta):
    n = x.shape[0]
    tm = TM_FFN
    row = pl.BlockSpec((tm, D_MODEL), lambda i: (i, 0))
    full = lambda arr: pl.BlockSpec(arr.shape, lambda i: (0, 0))
    return pl.pallas_call(
        _ffn_kernel,
        grid=(n // tm,),
        in_specs=[row, full(w1), full(w2), full(g), full(beta)],
        out_specs=row,
        out_shape=jax.ShapeDtypeStruct((n, D_MODEL), F32),
        compiler_params=pltpu.CompilerParams(
            dimension_semantics=("parallel",), vmem_limit_bytes=VMEM_LIMIT),
        name="ffn_ln",
    )(x, w1, w2, g, beta)


def kernel(x, w_in, b_gate, rel_bias, w_proj_a, w_proj_b, w_out, ln1_g, ln1_b, w_ffn_in,
           w_ffn_out, ln2_g, ln2_b):
    b, seq, d = x.shape
    n = b * seq
    depth = w_in.shape[0]
    h = x.reshape(n, d)
    for l in range(depth):
        qa, ka, va, qb, kb, vb, ga, gb = _inproj(
            h, w_in[l].astype(BF16), b_gate[l].reshape(1, -1))
        shp = (b, seq, -1)
        att_a = _mixer_a(qa.reshape(shp), ka.reshape(shp), va.reshape(shp),
                         _extended_bias(rel_bias[l], TQ_A))
        att_b = _mixer_b(qb.reshape(shp), kb.reshape(shp), vb.reshape(shp))
        h = _merge(h, att_a.reshape(n, -1), att_b.reshape(n, -1), ga, gb,
                   w_proj_a[l].astype(BF16), w_proj_b[l].astype(BF16), w_out[l].astype(BF16),
                   ln1_g[l].reshape(1, -1), ln1_b[l].reshape(1, -1))
        h = _ffn(h, w_ffn_in[l].astype(BF16), w_ffn_out[l].astype(BF16),
                 ln2_g[l].reshape(1, -1), ln2_b[l].reshape(1, -1))
    return h.reshape(b, seq, d)
```

```python
import functools
import math

import jax
import jax.numpy as jnp
from jax import lax
from jax.experimental import pallas as pl
from jax.experimental.pallas import tpu as pltpu

D_MODEL = 1024
HEAD_DIM = 64
WIDTH_A = 512
WIDTH_B = 512
CHUNK = 64
LEFT_CHUNKS = 8
REL_CLIP = 256
D_FF = 2816
DEPTH = 2
DEEPNORM_ALPHA = (2 * DEPTH) ** 0.25
LN_EPS = 1e-5
LOG2E = 1.4426950408889634
Q_SCALE = LOG2E / math.sqrt(HEAD_DIM)
NEG_BIG = -1e30

LANES = 128
HEADS_PER_BLOCK = LANES // HEAD_DIM
VMEM_LIMIT = 56 * 1024 * 1024

TM_PROJ = 512
TM_MERGE = 512
TM_FFN = 256
TQ_A = 256
KB_A = 128
TQ_B = 1024
TK_B = 256

BF16 = jnp.bfloat16
F32 = jnp.float32


def _dot(a, b):
    return jnp.dot(a, b, preferred_element_type=F32)


def _dot_nt(a, b):
    return lax.dot_general(a, b, (((1,), (1,)), ((), ())), preferred_element_type=F32)


def _layer_norm(r, g, b):
    mu = jnp.mean(r, axis=-1, keepdims=True)
    d = r - mu
    var = jnp.mean(d * d, axis=-1, keepdims=True)
    return d * lax.rsqrt(var + LN_EPS) * g + b


def _sigmoid(v):
    return 1.0 / (1.0 + jnp.exp(-v))


def _inproj_kernel(x_ref, w_ref, bg_ref, qa_ref, ka_ref, va_ref, qb_ref, kb_ref, vb_ref,
                   ga_ref, gb_ref):
    xb = x_ref[...].astype(BF16)

    def proj(c0, width):
        return _dot(xb, w_ref[:, c0:c0 + width])

    c = 0
    qa_ref[...] = (proj(c, WIDTH_A) * Q_SCALE).astype(BF16)
    c += WIDTH_A
    ka_ref[...] = proj(c, WIDTH_A).astype(BF16)
    c += WIDTH_A
    va_ref[...] = proj(c, WIDTH_A).astype(BF16)
    c += WIDTH_A
    qb_ref[...] = (proj(c, WIDTH_B) * Q_SCALE).astype(BF16)
    c += WIDTH_B
    kb_ref[...] = proj(c, WIDTH_B).astype(BF16)
    c += WIDTH_B
    vb_ref[...] = proj(c, WIDTH_B).astype(BF16)
    c += WIDTH_B
    ga_ref[...] = _sigmoid(proj(c, D_MODEL) + bg_ref[:, :D_MODEL]).astype(BF16)
    c += D_MODEL
    gb_ref[...] = _sigmoid(proj(c, D_MODEL) + bg_ref[:, D_MODEL:]).astype(BF16)


def _inproj(x, w, bg):
    n = x.shape[0]
    cols = w.shape[1]
    widths = [WIDTH_A] * 3 + [WIDTH_B] * 3 + [D_MODEL] * 2
    return pl.pallas_call(
        _inproj_kernel,
        grid=(n // TM_PROJ,),
        in_specs=[
            pl.BlockSpec((TM_PROJ, D_MODEL), lambda i: (i, 0)),
            pl.BlockSpec((D_MODEL, cols), lambda i: (0, 0)),
            pl.BlockSpec((1, 2 * D_MODEL), lambda i: (0, 0)),
        ],
        out_specs=[pl.BlockSpec((TM_PROJ, wd), lambda i: (i, 0)) for wd in widths],
        out_shape=[jax.ShapeDtypeStruct((n, wd), BF16) for wd in widths],
        compiler_params=pltpu.CompilerParams(
            dimension_semantics=("parallel",), vmem_limit_bytes=VMEM_LIMIT),
        name="inproj",
    )(x, w, bg)


def _mixer_a_kernel(q_ref, k_ref, v_ref, eb_ref, o_ref, bias_ref, *, tq, seq):
    left = LEFT_CHUNKS * CHUNK
    win = left + tq
    nkb = win // KB_A
    hp = pl.program_id(1)
    qi = pl.program_id(2)
    t0 = qi * tq

    @pl.when(qi == 0)
    def _build_bias():
        qc = lax.broadcasted_iota(jnp.int32, (tq, win), 0) // CHUNK
        kc = lax.broadcasted_iota(jnp.int32, (tq, win), 1) // CHUNK
        band = (kc >= qc) & (kc <= qc + LEFT_CHUNKS)
        for h in range(HEADS_PER_BLOCK):
            e = eb_ref[pl.ds(hp * HEADS_PER_BLOCK + h, 1), :]
            rows = jnp.broadcast_to(e, (tq, e.shape[1]))
            toep = pltpu.roll(rows, 0, 1, stride=1, stride_axis=0)[:, :win]
            bias_ref[h] = jnp.where(band, toep * LOG2E, NEG_BIG)

    q = q_ref[0]
    lane = lax.broadcasted_iota(jnp.int32, q.shape, 1)
    outs = []
    for h in range(HEADS_PER_BLOCK):
        in_head = (lane >= h * HEAD_DIM) & (lane < (h + 1) * HEAD_DIM)
        qm = jnp.where(in_head, q, jnp.zeros_like(q))
        s_blocks = []
        v_blocks = []
        for kb in range(nkb):
            start = t0 - left + kb * KB_A
            ok = start >= 0
            st = pl.multiple_of(jnp.maximum(start, 0), KB_A)
            kblk = k_ref[0, pl.ds(st, KB_A), :]
            v_blocks.append(v_ref[0, pl.ds(st, KB_A), :])
            s = _dot_nt(qm, kblk) + bias_ref[h, :, kb * KB_A:(kb + 1) * KB_A]
            s_blocks.append(jnp.where(ok, s, NEG_BIG))
        m = s_blocks[0].max(axis=-1, keepdims=True)
        for s in s_blocks[1:]:
            m = jnp.maximum(m, s.max(axis=-1, keepdims=True))
        l = jnp.zeros((tq, 1), F32)
        acc = jnp.zeros((tq, LANES), F32)
        for s, vblk in zip(s_blocks, v_blocks):
            p = jnp.exp2(s - m)
            l = l + p.sum(axis=-1, keepdims=True)
            acc = acc + _dot(p.astype(BF16), vblk)
        outs.append(acc / l)
    out = outs[0]
    for h in range(1, HEADS_PER_BLOCK):
        out = jnp.where(lane >= h * HEAD_DIM, outs[h], out)
    o_ref[0] = out.astype(o_ref.dtype)


def _mixer_a(q, k, v, ebias):
    b, seq, width = q.shape
    tq = TQ_A
    win = LEFT_CHUNKS * CHUNK + tq
    kern = functools.partial(_mixer_a_kernel, tq=tq, seq=seq)
    return pl.pallas_call(
        kern,
        grid=(b, width // LANES, seq // tq),
        in_specs=[
            pl.BlockSpec((1, tq, LANES), lambda bi, hp, qi: (bi, qi, hp)),
            pl.BlockSpec((1, seq, LANES), lambda bi, hp, qi: (bi, 0, hp)),
            pl.BlockSpec((1, seq, LANES), lambda bi, hp, qi: (bi, 0, hp)),
            pl.BlockSpec(ebias.shape, lambda bi, hp, qi: (0, 0)),
        ],
        out_specs=pl.BlockSpec((1, tq, LANES), lambda bi, hp, qi: (bi, qi, hp)),
        out_shape=jax.ShapeDtypeStruct((b, seq, width), BF16),
        scratch_shapes=[pltpu.VMEM((HEADS_PER_BLOCK, tq, win), F32)],
        compiler_params=pltpu.CompilerParams(
            dimension_semantics=("parallel", "parallel", "arbitrary"),
            vmem_limit_bytes=VMEM_LIMIT),
        name="mixer_a",
    )(q, k, v, ebias)


def _extended_bias(rel_bias, tq):
    left = LEFT_CHUNKS * CHUNK
    top = rel_bias[:, 2 * REL_CLIP:]
    n_flat = left - REL_CLIP + 1
    n_rev = left + tq - n_flat
    lo = 2 * REL_CLIP - n_rev
    rev = rel_bias[:, lo:2 * REL_CLIP][:, ::-1]
    h = rel_bias.shape[0]
    return jnp.concatenate(
        [jnp.broadcast_to(top, (h, n_flat)), rev, jnp.broadcast_to(top, (h, tq))], axis=1)


def _softplus2(z):
    return jnp.maximum(z, 0.0) + jnp.log2(1.0 + jnp.exp2(-jnp.abs(z)))


def _split_hi_lo(x):
    hi = x.astype(BF16)
    lo = (x - hi.astype(F32)).astype(BF16)
    return jnp.concatenate([hi, lo], axis=1)


def _mixer_b_kernel(q_ref, k_ref, v_ref, negl_ref, o_ref, acc_ref, carry_ref, *, tq, tk):
    qi = pl.program_id(2)
    t0 = pl.multiple_of(qi * tq, tq)
    q = q_ref[0]
    lane = lax.broadcasted_iota(jnp.int32, q.shape, 1)
    qms = []
    for h in range(HEADS_PER_BLOCK):
        in_head = (lane >= h * HEAD_DIM) & (lane < (h + 1) * HEAD_DIM)
        qms.append(jnp.where(in_head, q, jnp.zeros_like(q)))
    negl = negl_ref[...]
    reps = tk // LANES
    n_diag = tq // tk

    acc_ref[...] = jnp.zeros_like(acc_ref)
    carry_ref[...] = jnp.zeros_like(carry_ref)

    def tile(h, r0, kt, vt, causal):
        z = _dot_nt(qms[h][r0:], kt)
        sp = _softplus2(z)
        if causal is not None:
            sp = jnp.where(causal, sp, 0.0)
        negc = _dot(_split_hi_lo(sp), negl)
        tot = negc + jnp.concatenate([carry_ref[h, r0:, :]] * reps, axis=1)
        w = jnp.exp2(z + tot)
        if causal is not None:
            w = jnp.where(causal, w, 0.0)
        acc_ref[h, r0:, :] += _dot(w.astype(BF16), vt)
        carry_ref[h, r0:, :] = jnp.broadcast_to(tot[:, 0:1], (tq - r0, LANES))

    for j in reversed(range(n_diag)):
        r0 = j * tk
        row = lax.broadcasted_iota(jnp.int32, (tq - r0, tk), 0)
        col = lax.broadcasted_iota(jnp.int32, (tq - r0, tk), 1)
        causal = col < row
        kd = k_ref[0, pl.ds(t0 + r0, tk), :]
        vd = v_ref[0, pl.ds(t0 + r0, tk), :]
        for h in range(HEADS_PER_BLOCK):
            tile(h, r0, kd, vd, causal)

    n_full = qi * n_diag

    def body(it, _):
        k0 = pl.multiple_of((n_full - 1 - it) * tk, tk)
        kt = k_ref[0, pl.ds(k0, tk), :]
        vt = v_ref[0, pl.ds(k0, tk), :]
        for h in range(HEADS_PER_BLOCK):
            tile(h, 0, kt, vt, None)
        return 0

    lax.fori_loop(0, n_full, body, 0)

    out = acc_ref[0]
    for h in range(1, HEADS_PER_BLOCK):
        out = jnp.where(lane >= h * HEAD_DIM, acc_ref[h], out)
    o_ref[0] = out.astype(o_ref.dtype)


def _mixer_b(q, k, v):
    b, seq, width = q.shape
    tq, tk = TQ_B, TK_B
    idx = jnp.arange(tk)
    tri = jnp.where(idx[:, None] >= idx[None, :], -1.0, 0.0).astype(BF16)
    negl = jnp.concatenate([tri, tri], axis=0)
    kern = functools.partial(_mixer_b_kernel, tq=tq, tk=tk)
    return pl.pallas_call(
        kern,
        grid=(b, width // LANES, seq // tq),
        in_specs=[
            pl.BlockSpec((1, tq, LANES), lambda bi, hp, qi: (bi, qi, hp)),
            pl.BlockSpec((1, seq, LANES), lambda bi, hp, qi: (bi, 0, hp)),
            pl.BlockSpec((1, seq, LANES), lambda bi, hp, qi: (bi, 0, hp)),
            pl.BlockSpec((2 * tk, tk), lambda bi, hp, qi: (0, 0)),
        ],
        out_specs=pl.BlockSpec((1, tq, LANES), lambda bi, hp, qi: (bi, qi, hp)),
        out_shape=jax.ShapeDtypeStruct((b, seq, width), BF16),
        scratch_shapes=[pltpu.VMEM((HEADS_PER_BLOCK, tq, LANES), F32),
                        pltpu.VMEM((HEADS_PER_BLOCK, tq, LANES), F32)],
        compiler_params=pltpu.CompilerParams(
            dimension_semantics=("parallel", "parallel", "arbitrary"),
            vmem_limit_bytes=VMEM_LIMIT),
        name="mixer_b",
    )(q, k, v, negl)


def _merge_kernel(x_ref, a_ref, b_ref, ga_ref, gb_ref, wa_ref, wb_ref, wo_ref, g_ref, beta_ref,
                  o_ref):
    ya = _dot(a_ref[...], wa_ref[...])
    yb = _dot(b_ref[...], wb_ref[...])
    m = ga_ref[...].astype(F32) * ya + gb_ref[...].astype(F32) * yb
    mix = _dot(m.astype(BF16), wo_ref[...])
    r = DEEPNORM_ALPHA * x_ref[...] + mix
    o_ref[...] = _layer_norm(r, g_ref[...], beta_ref[...])


def _merge(x, a, bm, ga, gb, wa, wb, wo, g, beta):
    n = x.shape[0]
    tm = TM_MERGE
    row = lambda wd: pl.BlockSpec((tm, wd), lambda i: (i, 0))
    full = lambda arr: pl.BlockSpec(arr.shape, lambda i: (0, 0))
    return pl.pallas_call(
        _merge_kernel,
        grid=(n // tm,),
        in_specs=[row(D_MODEL), row(WIDTH_A), row(WIDTH_B), row(D_MODEL), row(D_MODEL),
                  full(wa), full(wb), full(wo), full(g), full(beta)],
        out_specs=row(D_MODEL),
        out_shape=jax.ShapeDtypeStruct((n, D_MODEL), F32),
        compiler_params=pltpu.CompilerParams(
            dimension_semantics=("parallel",), vmem_limit_bytes=VMEM_LIMIT),
        name="merge_ln",
    )(x, a, bm, ga, gb, wa, wb, wo, g, beta)


def _ffn_kernel(x_ref, w1_ref, w2_ref, g_ref, beta_ref, o_ref):
    x = x_ref[...]
    xb = x.astype(BF16)
    gate = _dot(xb, w1_ref[:, :D_FF])
    up = _dot(xb, w1_ref[:, D_FF:])
    act = gate * _sigmoid(gate) * up
    ffn = _dot(act.astype(BF16), w2_ref[...])
    r = DEEPNORM_ALPHA * x + ffn
    o_ref[...] = _layer_norm(r, g_ref[...], beta_ref[...])


def _ffn(x, w1, w2, g, beta):
    n = x.shape[0]
    tm = TM_FFN
    row = pl.BlockSpec((tm, D_MODEL), lambda i: (i, 0))
    full = lambda arr: pl.BlockSpec(arr.shape, lambda i: (0, 0))
    return pl.pallas_call(
        _ffn_kernel,
        grid=(n // tm,),
        in_specs=[row, full(w1), full(w2), full(g), full(beta)],
        out_specs=row,
        out_shape=jax.ShapeDtypeStruct((n, D_MODEL), F32),
        compiler_params=pltpu.CompilerParams(
            dimension_semantics=("parallel",), vmem_limit_bytes=VMEM_LIMIT),
        name="ffn_ln",
    )(x, w1, w2, g, beta)


def kernel(x, w_in, b_gate, rel_bias, w_proj_a, w_proj_b, w_out, ln1_g, ln1_b, w_ffn_in,
           w_ffn_out, ln2_g, ln2_b):
    b, seq, d = x.shape
    n = b * seq
    depth = w_in.shape[0]
    h = x.reshape(n, d)
    for l in range(depth):
        qa, ka, va, qb, kb, vb, ga, gb = _inproj(
            h, w_in[l].astype(BF16), b_gate[l].reshape(1, -1))
        shp = (b, seq, -1)
        att_a = _mixer_a(qa.reshape(shp), ka.reshape(shp), va.reshape(shp),
                         _extended_bias(rel_bias[l], TQ_A))
        att_b = _mixer_b(qb.reshape(shp), kb.reshape(shp), vb.reshape(shp))
        h = _merge(h, att_a.reshape(n, -1), att_b.reshape(n, -1), ga, gb,
                   w_proj_a[l].astype(BF16), w_proj_b[l].astype(BF16), w_out[l].astype(BF16),
                   ln1_g[l].reshape(1, -1), ln1_b[l].reshape(1, -1))
        h = _ffn(h, w_ffn_in[l].astype(BF16), w_ffn_out[l].astype(BF16),
                 ln2_g[l].reshape(1, -1), ln2_b[l].reshape(1, -1))
    return h.reshape(b, seq, d)
```

```python
import functools
import math

import jax
import jax.numpy as jnp
from jax import lax
from jax.experimental import pallas as pl
from jax.experimental.pallas import tpu as pltpu

D_MODEL = 1024
HEAD_DIM = 64
WIDTH_A = 512
WIDTH_B = 512
CHUNK = 64
LEFT_CHUNKS = 8
REL_CLIP = 256
D_FF = 2816
DEPTH = 2
DEEPNORM_ALPHA = (2 * DEPTH) ** 0.25
LN_EPS = 1e-5
LOG2E = 1.4426950408889634
Q_SCALE = LOG2E / math.sqrt(HEAD_DIM)
NEG_BIG = -1e30

LANES = 128
HEADS_PER_BLOCK = LANES // HEAD_DIM
VMEM_LIMIT = 56 * 1024 * 1024

TM_PROJ = 512
TM_MERGE = 512
TM_FFN = 256
TQ_A = 256
KB_A = 128
TQ_B = 1024
TK_B = 256

BF16 = jnp.bfloat16
F32 = jnp.float32


def _dot(a, b):
    return jnp.dot(a, b, preferred_element_type=F32)


def _dot_nt(a, b):
    return lax.dot_general(a, b, (((1,), (1,)), ((), ())), preferred_element_type=F32)


def _layer_norm(r, g, b):
    mu = jnp.mean(r, axis=-1, keepdims=True)
    d = r - mu
    var = jnp.mean(d * d, axis=-1, keepdims=True)
    return d * lax.rsqrt(var + LN_EPS) * g + b


def _sigmoid(v):
    return 1.0 / (1.0 + jnp.exp(-v))


def _inproj_kernel(x_ref, w_ref, bg_ref, qa_ref, ka_ref, va_ref, qb_ref, kb_ref, vb_ref,
                   ga_ref, gb_ref):
    xb = x_ref[...].astype(BF16)

    def proj(c0, width):
        return _dot(xb, w_ref[:, c0:c0 + width])

    c = 0
    qa_ref[...] = (proj(c, WIDTH_A) * Q_SCALE).astype(BF16)
    c += WIDTH_A
    ka_ref[...] = proj(c, WIDTH_A).astype(BF16)
    c += WIDTH_A
    va_ref[...] = proj(c, WIDTH_A).astype(BF16)
    c += WIDTH_A
    qb_ref[...] = (proj(c, WIDTH_B) * Q_SCALE).astype(BF16)
    c += WIDTH_B
    kb_ref[...] = proj(c, WIDTH_B).astype(BF16)
    c += WIDTH_B
    vb_ref[...] = proj(c, WIDTH_B).astype(BF16)
    c += WIDTH_B
    ga_ref[...] = _sigmoid(proj(c, D_MODEL) + bg_ref[:, :D_MODEL]).astype(BF16)
    c += D_MODEL
    gb_ref[...] = _sigmoid(proj(c, D_MODEL) + bg_ref[:, D_MODEL:]).astype(BF16)


def _inproj(x, w, bg):
    n = x.shape[0]
    cols = w.shape[1]
    widths = [WIDTH_A] * 3 + [WIDTH_B] * 3 + [D_MODEL] * 2
    return pl.pallas_call(
        _inproj_kernel,
        grid=(n // TM_PROJ,),
        in_specs=[
            pl.BlockSpec((TM_PROJ, D_MODEL), lambda i: (i, 0)),
            pl.BlockSpec((D_MODEL, cols), lambda i: (0, 0)),
            pl.BlockSpec((1, 2 * D_MODEL), lambda i: (0, 0)),
        ],
        out_specs=[pl.BlockSpec((TM_PROJ, wd), lambda i: (i, 0)) for wd in widths],
        out_shape=[jax.ShapeDtypeStruct((n, wd), BF16) for wd in widths],
        compiler_params=pltpu.CompilerParams(
            dimension_semantics=("parallel",), vmem_limit_bytes=VMEM_LIMIT),
        name="inproj",
    )(x, w, bg)


def _mixer_a_kernel(q_ref, k_ref, v_ref, eb_ref, o_ref, bias_ref, *, tq, seq):
    left = LEFT_CHUNKS * CHUNK
    win = left + tq
    nkb = win // KB_A
    hp = pl.program_id(1)
    qi = pl.program_id(2)
    t0 = qi * tq

    @pl.when(qi == 0)
    def _build_bias():
        qc = lax.broadcasted_iota(jnp.int32, (tq, win), 0) // CHUNK
        kc = lax.broadcasted_iota(jnp.int32, (tq, win), 1) // CHUNK
        band = (kc >= qc) & (kc <= qc + LEFT_CHUNKS)
        for h in range(HEADS_PER_BLOCK):
            e = eb_ref[pl.ds(hp * HEADS_PER_BLOCK + h, 1), :]
            rows = jnp.broadcast_to(e, (tq, e.shape[1]))
            toep = pltpu.roll(rows, 0, 1, stride=1, stride_axis=0)[:, :win]
            bias_ref[h] = jnp.where(band, toep * LOG2E, NEG_BIG)

    q = q_ref[0]
    lane = lax.broadcasted_iota(jnp.int32, q.shape, 1)
    outs = []
    for h in range(HEADS_PER_BLOCK):
        in_head = (lane >= h * HEAD_DIM) & (lane < (h + 1) * HEAD_DIM)
        qm = jnp.where(in_head, q, jnp.zeros_like(q))
        s_blocks = []
        v_blocks = []
        for kb in range(nkb):
            start = t0 - left + kb * KB_A
            ok = start >= 0
            st = pl.multiple_of(jnp.maximum(start, 0), KB_A)
            kblk = k_ref[0, pl.ds(st, KB_A), :]
            v_blocks.append(v_ref[0, pl.ds(st, KB_A), :])
            s = _dot_nt(qm, kblk) + bias_ref[h, :, kb * KB_A:(kb + 1) * KB_A]
            s_blocks.append(jnp.where(ok, s, NEG_BIG))
        m = s_blocks[0].max(axis=-1, keepdims=True)
        for s in s_blocks[1:]:
            m = jnp.maximum(m, s.max(axis=-1, keepdims=True))
        l = jnp.zeros((tq, 1), F32)
        acc = jnp.zeros((tq, LANES), F32)
        for s, vblk in zip(s_blocks, v_blocks):
            p = jnp.exp2(s - m)
            l = l + p.sum(axis=-1, keepdims=True)
            acc = acc + _dot(p.astype(BF16), vblk)
        outs.append(acc / l)
    out = outs[0]
    for h in range(1, HEADS_PER_BLOCK):
        out = jnp.where(lane >= h * HEAD_DIM, outs[h], out)
    o_ref[0] = out.astype(o_ref.dtype)


def _mixer_a(q, k, v, ebias):
    b, seq, width = q.shape
    tq = TQ_A
    win = LEFT_CHUNKS * CHUNK + tq
    kern = functools.partial(_mixer_a_kernel, tq=tq, seq=seq)
    return pl.pallas_call(
        kern,
        grid=(b, width // LANES, seq // tq),
        in_specs=[
            pl.BlockSpec((1, tq, LANES), lambda bi, hp, qi: (bi, qi, hp)),
            pl.BlockSpec((1, seq, LANES), lambda bi, hp, qi: (bi, 0, hp)),
            pl.BlockSpec((1, seq, LANES), lambda bi, hp, qi: (bi, 0, hp)),
            pl.BlockSpec(ebias.shape, lambda bi, hp, qi: (0, 0)),
        ],
        out_specs=pl.BlockSpec((1, tq, LANES), lambda bi, hp, qi: (bi, qi, hp)),
        out_shape=jax.ShapeDtypeStruct((b, seq, width), BF16),
        scratch_shapes=[pltpu.VMEM((HEADS_PER_BLOCK, tq, win), F32)],
        compiler_params=pltpu.CompilerParams(
            dimension_semantics=("parallel", "parallel", "arbitrary"),
            vmem_limit_bytes=VMEM_LIMIT),
        name="mixer_a",
    )(q, k, v, ebias)


def _extended_bias(rel_bias, tq):
    left = LEFT_CHUNKS * CHUNK
    top = rel_bias[:, 2 * REL_CLIP:]
    n_flat = left - REL_CLIP + 1
    n_rev = left + tq - n_flat
    lo = 2 * REL_CLIP - n_rev
    rev = rel_bias[:, lo:2 * REL_CLIP][:, ::-1]
    h = rel_bias.shape[0]
    return jnp.concatenate(
        [jnp.broadcast_to(top, (h, n_flat)), rev, jnp.broadcast_to(top, (h, tq))], axis=1)


EXP2_CLAMP = 126.0


def _softplus2(z):
    return jnp.maximum(z, jnp.log(1.0 + jnp.exp2(jnp.minimum(z, EXP2_CLAMP))) * LOG2E)


def _mixer_b_kernel(q_ref, k_ref, v_ref, negl_ref, o_ref, acc_ref, carry_ref, *, tq, tk):
    qi = pl.program_id(2)
    t0 = pl.multiple_of(qi * tq, tq)
    q = q_ref[0]
    lane = lax.broadcasted_iota(jnp.int32, q.shape, 1)
    qms = []
    for h in range(HEADS_PER_BLOCK):
        in_head = (lane >= h * HEAD_DIM) & (lane < (h + 1) * HEAD_DIM)
        qms.append(jnp.where(in_head, q, jnp.zeros_like(q)))
    negl = negl_ref[...]
    reps = tk // LANES
    n_diag = tq // tk

    acc_ref[...] = jnp.zeros_like(acc_ref)
    carry_ref[...] = jnp.zeros_like(carry_ref)

    def tile(h, r0, kt, vt, causal):
        z = _dot_nt(qms[h][r0:], kt)
        sp = _softplus2(z)
        if causal is not None:
            sp = jnp.where(causal, sp, 0.0)
        negc = _dot(sp.astype(BF16), negl)
        tot = negc + jnp.concatenate([carry_ref[h, r0:, :]] * reps, axis=1)
        w = jnp.exp2(z + tot)
        if causal is not None:
            w = jnp.where(causal, w, 0.0)
        acc_ref[h, r0:, :] += _dot(w.astype(BF16), vt)
        carry_ref[h, r0:, :] = jnp.broadcast_to(tot[:, 0:1], (tq - r0, LANES))

    for j in reversed(range(n_diag)):
        r0 = j * tk
        row = lax.broadcasted_iota(jnp.int32, (tq - r0, tk), 0)
        col = lax.broadcasted_iota(jnp.int32, (tq - r0, tk), 1)
        causal = col < row
        kd = k_ref[0, pl.ds(t0 + r0, tk), :]
        vd = v_ref[0, pl.ds(t0 + r0, tk), :]
        for h in range(HEADS_PER_BLOCK):
            tile(h, r0, kd, vd, causal)

    n_full = qi * n_diag

    def body(it, _):
        k0 = pl.multiple_of((n_full - 1 - it) * tk, tk)
        kt = k_ref[0, pl.ds(k0, tk), :]
        vt = v_ref[0, pl.ds(k0, tk), :]
        for h in range(HEADS_PER_BLOCK):
            tile(h, 0, kt, vt, None)
        return 0

    lax.fori_loop(0, n_full, body, 0)

    out = acc_ref[0]
    for h in range(1, HEADS_PER_BLOCK):
        out = jnp.where(lane >= h * HEAD_DIM, acc_ref[h], out)
    o_ref[0] = out.astype(o_ref.dtype)


def _mixer_b(q, k, v):
    b, seq, width = q.shape
    tq, tk = TQ_B, TK_B
    idx = jnp.arange(tk)
    tri = jnp.where(idx[:, None] >= idx[None, :], -1.0, 0.0).astype(BF16)
    negl = tri
    kern = functools.partial(_mixer_b_kernel, tq=tq, tk=tk)
    return pl.pallas_call(
        kern,
        grid=(b, width // LANES, seq // tq),
        in_specs=[
            pl.BlockSpec((1, tq, LANES), lambda bi, hp, qi: (bi, qi, hp)),
            pl.BlockSpec((1, seq, LANES), lambda bi, hp, qi: (bi, 0, hp)),
            pl.BlockSpec((1, seq, LANES), lambda bi, hp, qi: (bi, 0, hp)),
            pl.BlockSpec((tk, tk), lambda bi, hp, qi: (0, 0)),
        ],
        out_specs=pl.BlockSpec((1, tq, LANES), lambda bi, hp, qi: (bi, qi, hp)),
        out_shape=jax.ShapeDtypeStruct((b, seq, width), BF16),
        scratch_shapes=[pltpu.VMEM((HEADS_PER_BLOCK, tq, LANES), F32),
                        pltpu.VMEM((HEADS_PER_BLOCK, tq, LANES), F32)],
        compiler_params=pltpu.CompilerParams(
            dimension_semantics=("parallel", "parallel", "arbitrary"),
            vmem_limit_bytes=VMEM_LIMIT),
        name="mixer_b",
    )(q, k, v, negl)


def _merge_kernel(x_ref, a_ref, b_ref, ga_ref, gb_ref, wa_ref, wb_ref, wo_ref, g_ref, beta_ref,
                  o_ref):
    ya = _dot(a_ref[...], wa_ref[...])
    yb = _dot(b_ref[...], wb_ref[...])
    m = ga_ref[...].astype(F32) * ya + gb_ref[...].astype(F32) * yb
    mix = _dot(m.astype(BF16), wo_ref[...])
    r = DEEPNORM_ALPHA * x_ref[...] + mix
    o_ref[...] = _layer_norm(r, g_ref[...], beta_ref[...])


def _merge(x, a, bm, ga, gb, wa, wb, wo, g, beta):
    n = x.shape[0]
    tm = TM_MERGE
    row = lambda wd: pl.BlockSpec((tm, wd), lambda i: (i, 0))
    full = lambda arr: pl.BlockSpec(arr.shape, lambda i: (0, 0))
    return pl.pallas_call(
        _merge_kernel,
        grid=(n // tm,),
        in_specs=[row(D_MODEL), row(WIDTH_A), row(WIDTH_B), row(D_MODEL), row(D_MODEL),
                  full(wa), full(wb), full(wo), full(g), full(beta)],
        out_specs=row(D_MODEL),
        out_shape=jax.ShapeDtypeStruct((n, D_MODEL), F32),
        compiler_params=pltpu.CompilerParams(
            dimension_semantics=("parallel",), vmem_limit_bytes=VMEM_LIMIT),
        name="merge_ln",
    )(x, a, bm, ga, gb, wa, wb, wo, g, beta)


def _ffn_kernel(x_ref, w1_ref, w2_ref, g_ref, beta_ref, o_ref):
    x = x_ref[...]
    xb = x.astype(BF16)
    gate = _dot(xb, w1_ref[:, :D_FF])
    up = _dot(xb, w1_ref[:, D_FF:])
    act = gate * _sigmoid(gate) * up
    ffn = _dot(act.astype(BF16), w2_ref[...])
    r = DEEPNORM_ALPHA * x + ffn
    o_ref[...] = _layer_norm(r, g_ref[...], beta_ref[...])


def _ffn(x, w1, w2, g, beta):
    n = x.shape[0]
    tm = TM_FFN
    row = pl.BlockSpec((tm, D_MODEL), lambda i: (i, 0))
    full = lambda arr: pl.BlockSpec(arr.shape, lambda i: (0, 0))
    return pl.pallas_call(
        _ffn_kernel,
        grid=(n // tm,),
        in_specs=[row, full(w1), full(w2), full(g), full(beta)],
        out_specs=row,
        out_shape=jax.ShapeDtypeStruct((n, D_MODEL), F32),
        compiler_params=pltpu.CompilerParams(
            dimension_semantics=("parallel",), vmem_limit_bytes=VMEM_LIMIT),
        name="ffn_ln",
    )(x, w1, w2, g, beta)


def kernel(x, w_in, b_gate, rel_bias, w_proj_a, w_proj_b, w_out, ln1_g, ln1_b, w_ffn_in,
           w_ffn_out, ln2_g, ln2_b):
    b, seq, d = x.shape
    n = b * seq
    depth = w_in.shape[0]
    h = x.reshape(n, d)
    for l in range(depth):
        qa, ka, va, qb, kb, vb, ga, gb = _inproj(
            h, w_in[l].astype(BF16), b_gate[l].reshape(1, -1))
        shp = (b, seq, -1)
        att_a = _mixer_a(qa.reshape(shp), ka.reshape(shp), va.reshape(shp),
                         _extended_bias(rel_bias[l], TQ_A))
        att_b = _mixer_b(qb.reshape(shp), kb.reshape(shp), vb.reshape(shp))
        h = _merge(h, att_a.reshape(n, -1), att_b.reshape(n, -1), ga, gb,
                   w_proj_a[l].astype(BF16), w_proj_b[l].astype(BF16), w_out[l].astype(BF16),
                   ln1_g[l].reshape(1, -1), ln1_b[l].reshape(1, -1))
        h = _ffn(h, w_ffn_in[l].astype(BF16), w_ffn_out[l].astype(BF16),
                 ln2_g[l].reshape(1, -1), ln2_b[l].reshape(1, -1))
    return h.reshape(b, seq, d)
```

```python
import functools
import math

import jax
import jax.numpy as jnp
from jax import lax
from jax.experimental import pallas as pl
from jax.experimental.pallas import tpu as pltpu

D_MODEL = 1024
HEAD_DIM = 64
WIDTH_A = 512
WIDTH_B = 512
CHUNK = 64
LEFT_CHUNKS = 8
REL_CLIP = 256
D_FF = 2816
DEPTH = 2
DEEPNORM_ALPHA = (2 * DEPTH) ** 0.25
LN_EPS = 1e-5
LOG2E = 1.4426950408889634
Q_SCALE = LOG2E / math.sqrt(HEAD_DIM)
NEG_BIG = -1e30

LANES = 128
HEADS_PER_BLOCK = LANES // HEAD_DIM
VMEM_LIMIT = 56 * 1024 * 1024

TM_PROJ = 512
TM_MERGE = 512
TM_FFN = 256
TQ_A = 256
NSUB_A = 4
KB_A = 256
TQ_B = 1024
TK_B = 256
KGROUP_B = 2

BF16 = jnp.bfloat16
F32 = jnp.float32


def _dot(a, b):
    return jnp.dot(a, b, preferred_element_type=F32)


def _dot_nt(a, b):
    return lax.dot_general(a, b, (((1,), (1,)), ((), ())), preferred_element_type=F32)


def _layer_norm(r, g, b):
    mu = jnp.mean(r, axis=-1, keepdims=True)
    d = r - mu
    var = jnp.mean(d * d, axis=-1, keepdims=True)
    return d * lax.rsqrt(var + LN_EPS) * g + b


def _sigmoid(v):
    return 1.0 / (1.0 + jnp.exp(-v))


def _inproj_kernel(x_ref, w_ref, bg_ref, qa_ref, ka_ref, va_ref, qb_ref, kb_ref, vb_ref,
                   ga_ref, gb_ref):
    xb = x_ref[...].astype(BF16)

    def proj(c0, width):
        return _dot(xb, w_ref[:, c0:c0 + width])

    c = 0
    qa_ref[...] = (proj(c, WIDTH_A) * Q_SCALE).astype(BF16)
    c += WIDTH_A
    ka_ref[...] = proj(c, WIDTH_A).astype(BF16)
    c += WIDTH_A
    va_ref[...] = proj(c, WIDTH_A).astype(BF16)
    c += WIDTH_A
    qb_ref[...] = (proj(c, WIDTH_B) * Q_SCALE).astype(BF16)
    c += WIDTH_B
    kb_ref[...] = proj(c, WIDTH_B).astype(BF16)
    c += WIDTH_B
    vb_ref[...] = proj(c, WIDTH_B).astype(BF16)
    c += WIDTH_B
    ga_ref[...] = _sigmoid(proj(c, D_MODEL) + bg_ref[:, :D_MODEL]).astype(BF16)
    c += D_MODEL
    gb_ref[...] = _sigmoid(proj(c, D_MODEL) + bg_ref[:, D_MODEL:]).astype(BF16)


def _inproj(x, w, bg):
    n = x.shape[0]
    cols = w.shape[1]
    widths = [WIDTH_A] * 3 + [WIDTH_B] * 3 + [D_MODEL] * 2
    return pl.pallas_call(
        _inproj_kernel,
        grid=(n // TM_PROJ,),
        in_specs=[
            pl.BlockSpec((TM_PROJ, D_MODEL), lambda i: (i, 0)),
            pl.BlockSpec((D_MODEL, cols), lambda i: (0, 0)),
            pl.BlockSpec((1, 2 * D_MODEL), lambda i: (0, 0)),
        ],
        out_specs=[pl.BlockSpec((TM_PROJ, wd), lambda i: (i, 0)) for wd in widths],
        out_shape=[jax.ShapeDtypeStruct((n, wd), BF16) for wd in widths],
        compiler_params=pltpu.CompilerParams(
            dimension_semantics=("parallel",), vmem_limit_bytes=VMEM_LIMIT),
        name="inproj",
    )(x, w, bg)


def _mixer_a_kernel(q_ref, k_ref, v_ref, eb_ref, o_ref, bias_ref, *, tq, nsub):
    left = LEFT_CHUNKS * CHUNK
    win = left + tq
    nkb = win // KB_A
    hp = pl.program_id(1)
    qi = pl.program_id(2)

    @pl.when(qi == 0)
    def _build_bias():
        qc = lax.broadcasted_iota(jnp.int32, (tq, win), 0) // CHUNK
        kc = lax.broadcasted_iota(jnp.int32, (tq, win), 1) // CHUNK
        band = (kc >= qc) & (kc <= qc + LEFT_CHUNKS)
        for h in range(HEADS_PER_BLOCK):
            e = eb_ref[pl.ds(hp * HEADS_PER_BLOCK + h, 1), :]
            rows = jnp.broadcast_to(e, (tq, e.shape[1]))
            toep = pltpu.roll(rows, 0, 1, stride=1, stride_axis=0)[:, :win]
            bias_ref[h] = jnp.where(band, toep * LOG2E, NEG_BIG)

    assert HEADS_PER_BLOCK == 2 and tq == KB_A and tq * nsub >= left
    wb = left // KB_A + 1
    q = q_ref[0]
    lane = lax.broadcasted_iota(jnp.int32, q.shape, 1)
    vlane = lax.broadcasted_iota(jnp.int32, (KB_A, LANES), 1)
    first = lane < HEAD_DIM
    qms = [jnp.where(first, q, jnp.zeros_like(q)), jnp.where(first, jnp.zeros_like(q), q)]

    def step(first_block):
        t0 = qi * (nsub * tq)
        blocks = range(first_block, nsub + wb - 1)
        users = {b: range(max(0, b - wb + 1), min(nsub - 1, b) + 1) for b in blocks}
        starts = {b: pl.multiple_of(t0 - left + b * KB_A, KB_A) for b in blocks}

        s = {}
        for b in blocks:
            r0, r1 = users[b][0] * tq, (users[b][-1] + 1) * tq
            lhs = jnp.concatenate([qms[0][r0:r1], qms[1][r0:r1]], axis=0)
            sb = _dot_nt(lhs, k_ref[0, pl.ds(starts[b], KB_A), :])
            for h in range(HEADS_PER_BLOCK):
                for i in users[b]:
                    off = h * (r1 - r0) + (i - users[b][0]) * tq
                    kb = b - i
                    s[h, i, b] = sb[off:off + tq] + bias_ref[h, :, kb * KB_A:(kb + 1) * KB_A]

        p = {}
        for h in range(HEADS_PER_BLOCK):
            for i in range(nsub):
                mine = [s[h, i, b] for b in blocks if i in users[b]]
                m = mine[0]
                for sb in mine[1:]:
                    m = jnp.maximum(m, sb)
                for c in range(1, KB_A // LANES):
                    m = jnp.maximum(m[:, :LANES], m[:, c * LANES:(c + 1) * LANES])
                m = m.max(axis=-1, keepdims=True)
                for b in blocks:
                    if i in users[b]:
                        p[h, i, b] = jnp.exp2(s[h, i, b] - m).astype(BF16)

        acc = {}
        for b in blocks:
            vblk = v_ref[0, pl.ds(starts[b], KB_A), :]
            for h in range(HEADS_PER_BLOCK):
                mine = (vlane < HEAD_DIM) if h == 0 else (vlane >= HEAD_DIM)
                vaug = jnp.where(mine, vblk, jnp.ones_like(vblk))
                pv = _dot(jnp.concatenate([p[h, i, b] for i in users[b]], axis=0), vaug)
                for n, i in enumerate(users[b]):
                    part = pv[n * tq:(n + 1) * tq]
                    acc[h, i] = part if (h, i) not in acc else acc[h, i] + part

        fl = lax.broadcasted_iota(jnp.int32, (tq, LANES), 1) < HEAD_DIM
        for i in range(nsub):
            num = jnp.where(fl, acc[0, i], acc[1, i])
            den = pltpu.roll(jnp.where(fl, acc[1, i], acc[0, i]), HEAD_DIM, 1)
            o_ref[0, i * tq:(i + 1) * tq, :] = (num / den).astype(o_ref.dtype)

    pl.when(qi == 0)(functools.partial(step, left // KB_A))
    pl.when(qi != 0)(functools.partial(step, 0))


def _mixer_a(q, k, v, ebias):
    b, seq, width = q.shape
    tq, nsub = TQ_A, NSUB_A
    win = LEFT_CHUNKS * CHUNK + tq
    kern = functools.partial(_mixer_a_kernel, tq=tq, nsub=nsub)
    return pl.pallas_call(
        kern,
        grid=(b, width // LANES, seq // (tq * nsub)),
        in_specs=[
            pl.BlockSpec((1, tq * nsub, LANES), lambda bi, hp, qi: (bi, qi, hp)),
            pl.BlockSpec((1, seq, LANES), lambda bi, hp, qi: (bi, 0, hp)),
            pl.BlockSpec((1, seq, LANES), lambda bi, hp, qi: (bi, 0, hp)),
            pl.BlockSpec(ebias.shape, lambda bi, hp, qi: (0, 0)),
        ],
        out_specs=pl.BlockSpec((1, tq * nsub, LANES), lambda bi, hp, qi: (bi, qi, hp)),
        out_shape=jax.ShapeDtypeStruct((b, seq, width), BF16),
        scratch_shapes=[pltpu.VMEM((HEADS_PER_BLOCK, tq, win), F32)],
        compiler_params=pltpu.CompilerParams(
            dimension_semantics=("parallel", "parallel", "arbitrary"),
            vmem_limit_bytes=VMEM_LIMIT),
        name="mixer_a",
    )(q, k, v, ebias)


def _extended_bias(rel_bias, tq):
    left = LEFT_CHUNKS * CHUNK
    top = rel_bias[:, 2 * REL_CLIP:]
    n_flat = left - REL_CLIP + 1
    n_rev = left + tq - n_flat
    lo = 2 * REL_CLIP - n_rev
    rev = rel_bias[:, lo:2 * REL_CLIP][:, ::-1]
    h = rel_bias.shape[0]
    return jnp.concatenate(
        [jnp.broadcast_to(top, (h, n_flat)), rev, jnp.broadcast_to(top, (h, tq))], axis=1)


EXP2_CLAMP = 126.0


def _softplus2(z):
    return jnp.maximum(z, jnp.log(1.0 + jnp.exp2(jnp.minimum(z, EXP2_CLAMP))) * LOG2E)


def _mixer_b_kernel(q_ref, k_ref, v_ref, negl_ref, o_ref, acc_ref, carry_ref, *, tq, tk):
    qi = pl.program_id(2)
    t0 = pl.multiple_of(qi * tq, tq)
    q = q_ref[0]
    lane = lax.broadcasted_iota(jnp.int32, q.shape, 1)
    qms = []
    for h in range(HEADS_PER_BLOCK):
        in_head = (lane >= h * HEAD_DIM) & (lane < (h + 1) * HEAD_DIM)
        qms.append(jnp.where(in_head, q, jnp.zeros_like(q)))
    negl = negl_ref[...]
    reps = tk // LANES
    n_diag = tq // tk

    acc_ref[...] = jnp.zeros_like(acc_ref)
    carry_ref[...] = jnp.zeros_like(carry_ref)

    def tile(h, r0, kt, vt, causal):
        z = _dot_nt(qms[h][r0:], kt)
        sp = _softplus2(z)
        if causal is not None:
            sp = jnp.where(causal, sp, 0.0)
        spb = sp.astype(BF16)
        carry = carry_ref[h, r0:, :]
        tots = []
        for s in reversed(range(kt.shape[0] // tk)):
            negc = _dot(spb[:, s * tk:(s + 1) * tk], negl)
            tot = negc + jnp.concatenate([carry] * reps, axis=1)
            tots.insert(0, tot)
            carry = jnp.broadcast_to(tot[:, 0:1], (tq - r0, LANES))
        w = jnp.exp2(z + jnp.concatenate(tots, axis=1))
        if causal is not None:
            w = jnp.where(causal, w, 0.0)
        acc_ref[h, r0:, :] += _dot(w.astype(BF16), vt)
        carry_ref[h, r0:, :] = carry

    for j in reversed(range(n_diag)):
        r0 = j * tk
        row = lax.broadcasted_iota(jnp.int32, (tq - r0, tk), 0)
        col = lax.broadcasted_iota(jnp.int32, (tq - r0, tk), 1)
        causal = col < row
        kd = k_ref[0, pl.ds(t0 + r0, tk), :]
        vd = v_ref[0, pl.ds(t0 + r0, tk), :]
        for h in range(HEADS_PER_BLOCK):
            tile(h, r0, kd, vd, causal)

    kw = KGROUP_B * tk
    n_full = qi * (tq // kw)

    def body(it, _):
        k0 = pl.multiple_of((n_full - 1 - it) * kw, kw)
        kt = k_ref[0, pl.ds(k0, kw), :]
        vt = v_ref[0, pl.ds(k0, kw), :]
        for h in range(HEADS_PER_BLOCK):
            tile(h, 0, kt, vt, None)
        return 0

    lax.fori_loop(0, n_full, body, 0)

    out = acc_ref[0]
    for h in range(1, HEADS_PER_BLOCK):
        out = jnp.where(lane >= h * HEAD_DIM, acc_ref[h], out)
    o_ref[0] = out.astype(o_ref.dtype)


def _mixer_b(q, k, v):
    b, seq, width = q.shape
    tq, tk = TQ_B, TK_B
    idx = jnp.arange(tk)
    tri = jnp.where(idx[:, None] >= idx[None, :], -1.0, 0.0).astype(BF16)
    negl = tri
    kern = functools.partial(_mixer_b_kernel, tq=tq, tk=tk)
    return pl.pallas_call(
        kern,
        grid=(b, width // LANES, seq // tq),
        in_specs=[
            pl.BlockSpec((1, tq, LANES), lambda bi, hp, qi: (bi, qi, hp)),
            pl.BlockSpec((1, seq, LANES), lambda bi, hp, qi: (bi, 0, hp)),
            pl.BlockSpec((1, seq, LANES), lambda bi, hp, qi: (bi, 0, hp)),
            pl.BlockSpec((tk, tk), lambda bi, hp, qi: (0, 0)),
        ],
        out_specs=pl.BlockSpec((1, tq, LANES), lambda bi, hp, qi: (bi, qi, hp)),
        out_shape=jax.ShapeDtypeStruct((b, seq, width), BF16),
        scratch_shapes=[pltpu.VMEM((HEADS_PER_BLOCK, tq, LANES), F32),
                        pltpu.VMEM((HEADS_PER_BLOCK, tq, LANES), F32)],
        compiler_params=pltpu.CompilerParams(
            dimension_semantics=("parallel", "parallel", "arbitrary"),
            vmem_limit_bytes=VMEM_LIMIT),
        name="mixer_b",
    )(q, k, v, negl)


def _merge_kernel(x_ref, a_ref, b_ref, ga_ref, gb_ref, wa_ref, wb_ref, wo_ref, g_ref, beta_ref,
                  o_ref):
    ya = _dot(a_ref[...], wa_ref[...])
    yb = _dot(b_ref[...], wb_ref[...])
    m = ga_ref[...].astype(F32) * ya + gb_ref[...].astype(F32) * yb
    mix = _dot(m.astype(BF16), wo_ref[...])
    r = DEEPNORM_ALPHA * x_ref[...] + mix
    o_ref[...] = _layer_norm(r, g_ref[...], beta_ref[...])


def _merge(x, a, bm, ga, gb, wa, wb, wo, g, beta):
    n = x.shape[0]
    tm = TM_MERGE
    row = lambda wd: pl.BlockSpec((tm, wd), lambda i: (i, 0))
    full = lambda arr: pl.BlockSpec(arr.shape, lambda i: (0, 0))
    return pl.pallas_call(
        _merge_kernel,
        grid=(n // tm,),
        in_specs=[row(D_MODEL), row(WIDTH_A), row(WIDTH_B), row(D_MODEL), row(D_MODEL),
                  full(wa), full(wb), full(wo), full(g), full(beta)],
        out_specs=row(D_MODEL),
        out_shape=jax.ShapeDtypeStruct((n, D_MODEL), F32),
        compiler_params=pltpu.CompilerParams(
            dimension_semantics=("parallel",), vmem_limit_bytes=VMEM_LIMIT),
        name="merge_ln",
    )(x, a, bm, ga, gb, wa, wb, wo, g, beta)


def _ffn_kernel(x_ref, w1_ref, w2_ref, g_ref, beta_ref, o_ref):
    x = x_ref[...]
    xb = x.astype(BF16)
    gate = _dot(xb, w1_ref[:, :D_FF])
    up = _dot(xb, w1_ref[:, D_FF:])
    act = gate * _sigmoid(gate) * up
    ffn = _dot(act.astype(BF16), w2_ref[...])
    r = DEEPNORM_ALPHA * x + ffn
    o_ref[...] = _layer_norm(r, g_ref[...], beta_ref[...])


def _ffn(x, w1, w2, g, beta):
    n = x.shape[0]
    tm = TM_FFN
    row = pl.BlockSpec((tm, D_MODEL), lambda i: (i, 0))
    full = lambda arr: pl.BlockSpec(arr.shape, lambda i: (0, 0))
    return pl.pallas_call(
        _ffn_kernel,
        grid=(n // tm,),
        in_specs=[row, full(w1), full(w2), full(g), full(beta)],
        out_specs=row,
        out_shape=jax.ShapeDtypeStruct((n, D_MODEL), F32),
        compiler_params=pltpu.CompilerParams(
            dimension_semantics=("parallel",), vmem_limit_bytes=VMEM_LIMIT),
        name="ffn_ln",
    )(x, w1, w2, g, beta)


def kernel(x, w_in, b_gate, rel_bias, w_proj_a, w_proj_b, w_out, ln1_g, ln1_b, w_ffn_in,
           w_ffn_out, ln2_g, ln2_b):
    b, seq, d = x.shape
    n = b * seq
    depth = w_in.shape[0]
    h = x.reshape(n, d)
    for l in range(depth):
        qa, ka, va, qb, kb, vb, ga, gb = _inproj(
            h, w_in[l].astype(BF16), b_gate[l].reshape(1, -1))
        shp = (b, seq, -1)
        att_a = _mixer_a(qa.reshape(shp), ka.reshape(shp), va.reshape(shp),
                         _extended_bias(rel_bias[l], TQ_A))
        att_b = _mixer_b(qb.reshape(shp), kb.reshape(shp), vb.reshape(shp))
        h = _merge(h, att_a.reshape(n, -1), att_b.reshape(n, -1), ga, gb,
                   w_proj_a[l].astype(BF16), w_proj_b[l].astype(BF16), w_out[l].astype(BF16),
                   ln1_g[l].reshape(1, -1), ln1_b[l].reshape(1, -1))
        h = _ffn(h, w_ffn_in[l].astype(BF16), w_ffn_out[l].astype(BF16),
                 ln2_g[l].reshape(1, -1), ln2_b[l].reshape(1, -1))
    return h.reshape(b, seq, d)
```

```python
import functools
import math

import jax
import jax.numpy as jnp
from jax import lax
from jax.experimental import pallas as pl
from jax.experimental.pallas import tpu as pltpu

D_MODEL = 1024
HEAD_DIM = 64
WIDTH_A = 512
WIDTH_B = 512
CHUNK = 64
LEFT_CHUNKS = 8
REL_CLIP = 256
D_FF = 2816
DEPTH = 2
DEEPNORM_ALPHA = (2 * DEPTH) ** 0.25
LN_EPS = 1e-5
LOG2E = 1.4426950408889634
Q_SCALE = LOG2E / math.sqrt(HEAD_DIM)
NEG_BIG = -1e30

LANES = 128
HEADS_PER_BLOCK = LANES // HEAD_DIM
VMEM_LIMIT = 56 * 1024 * 1024

TM_PROJ = 512
TM_MERGE = 512
TM_FFN = 256
TQ_A = 256
NSUB_A = 4
KB_A = 256
TQ_B = 1024
TK_B = 256
CARRY_EXIT = -200.0

BF16 = jnp.bfloat16
F32 = jnp.float32


def _dot(a, b):
    return jnp.dot(a, b, preferred_element_type=F32)


def _dot_nt(a, b):
    return lax.dot_general(a, b, (((1,), (1,)), ((), ())), preferred_element_type=F32)


def _layer_norm(r, g, b):
    mu = jnp.mean(r, axis=-1, keepdims=True)
    d = r - mu
    var = jnp.mean(d * d, axis=-1, keepdims=True)
    return d * lax.rsqrt(var + LN_EPS) * g + b


def _sigmoid(v):
    return 1.0 / (1.0 + jnp.exp(-v))


def _inproj_kernel(x_ref, w_ref, bg_ref, qa_ref, ka_ref, va_ref, qb_ref, kb_ref, vb_ref,
                   ga_ref, gb_ref):
    xb = x_ref[...].astype(BF16)

    def proj(c0, width):
        return _dot(xb, w_ref[:, c0:c0 + width])

    c = 0
    qa_ref[...] = (proj(c, WIDTH_A) * Q_SCALE).astype(BF16)
    c += WIDTH_A
    ka_ref[...] = proj(c, WIDTH_A).astype(BF16)
    c += WIDTH_A
    va_ref[...] = proj(c, WIDTH_A).astype(BF16)
    c += WIDTH_A
    qb_ref[...] = (proj(c, WIDTH_B) * Q_SCALE).astype(BF16)
    c += WIDTH_B
    kb_ref[...] = proj(c, WIDTH_B).astype(BF16)
    c += WIDTH_B
    vb_ref[...] = proj(c, WIDTH_B).astype(BF16)
    c += WIDTH_B
    ga_ref[...] = _sigmoid(proj(c, D_MODEL) + bg_ref[:, :D_MODEL]).astype(BF16)
    c += D_MODEL
    gb_ref[...] = _sigmoid(proj(c, D_MODEL) + bg_ref[:, D_MODEL:]).astype(BF16)


def _inproj(x, w, bg):
    n = x.shape[0]
    cols = w.shape[1]
    widths = [WIDTH_A] * 3 + [WIDTH_B] * 3 + [D_MODEL] * 2
    return pl.pallas_call(
        _inproj_kernel,
        grid=(n // TM_PROJ,),
        in_specs=[
            pl.BlockSpec((TM_PROJ, D_MODEL), lambda i: (i, 0)),
            pl.BlockSpec((D_MODEL, cols), lambda i: (0, 0)),
            pl.BlockSpec((1, 2 * D_MODEL), lambda i: (0, 0)),
        ],
        out_specs=[pl.BlockSpec((TM_PROJ, wd), lambda i: (i, 0)) for wd in widths],
        out_shape=[jax.ShapeDtypeStruct((n, wd), BF16) for wd in widths],
        compiler_params=pltpu.CompilerParams(
            dimension_semantics=("parallel",), vmem_limit_bytes=VMEM_LIMIT),
        name="inproj",
    )(x, w, bg)


def _mixer_a_kernel(q_ref, k_ref, v_ref, eb_ref, o_ref, bias_ref, *, tq, nsub):
    left = LEFT_CHUNKS * CHUNK
    win = left + tq
    nkb = win // KB_A
    hp = pl.program_id(1)
    qi = pl.program_id(2)

    @pl.when(qi == 0)
    def _build_bias():
        qc = lax.broadcasted_iota(jnp.int32, (tq, win), 0) // CHUNK
        kc = lax.broadcasted_iota(jnp.int32, (tq, win), 1) // CHUNK
        band = (kc >= qc) & (kc <= qc + LEFT_CHUNKS)
        for h in range(HEADS_PER_BLOCK):
            e = eb_ref[pl.ds(hp * HEADS_PER_BLOCK + h, 1), :]
            rows = jnp.broadcast_to(e, (tq, e.shape[1]))
            toep = pltpu.roll(rows, 0, 1, stride=1, stride_axis=0)[:, :win]
            bias_ref[h] = jnp.where(band, toep * LOG2E, NEG_BIG)

    assert HEADS_PER_BLOCK == 2 and tq == KB_A and tq * nsub >= left
    wb = left // KB_A + 1
    q = q_ref[0]
    lane = lax.broadcasted_iota(jnp.int32, q.shape, 1)
    vlane = lax.broadcasted_iota(jnp.int32, (KB_A, LANES), 1)
    first = lane < HEAD_DIM
    qms = [jnp.where(first, q, jnp.zeros_like(q)), jnp.where(first, jnp.zeros_like(q), q)]

    def step(first_block):
        t0 = qi * (nsub * tq)
        blocks = range(first_block, nsub + wb - 1)
        users = {b: range(max(0, b - wb + 1), min(nsub - 1, b) + 1) for b in blocks}
        starts = {b: pl.multiple_of(t0 - left + b * KB_A, KB_A) for b in blocks}

        s = {}
        for b in blocks:
            r0, r1 = users[b][0] * tq, (users[b][-1] + 1) * tq
            lhs = jnp.concatenate([qms[0][r0:r1], qms[1][r0:r1]], axis=0)
            sb = _dot_nt(lhs, k_ref[0, pl.ds(starts[b], KB_A), :])
            for h in range(HEADS_PER_BLOCK):
                for i in users[b]:
                    off = h * (r1 - r0) + (i - users[b][0]) * tq
                    kb = b - i
                    s[h, i, b] = sb[off:off + tq] + bias_ref[h, :, kb * KB_A:(kb + 1) * KB_A]

        p = {}
        for h in range(HEADS_PER_BLOCK):
            for i in range(nsub):
                mine = [s[h, i, b] for b in blocks if i in users[b]]
                m = mine[0]
                for sb in mine[1:]:
                    m = jnp.maximum(m, sb)
                for c in range(1, KB_A // LANES):
                    m = jnp.maximum(m[:, :LANES], m[:, c * LANES:(c + 1) * LANES])
                m = m.max(axis=-1, keepdims=True)
                for b in blocks:
                    if i in users[b]:
                        p[h, i, b] = jnp.exp2(s[h, i, b] - m).astype(BF16)

        acc = {}
        for b in blocks:
            vblk = v_ref[0, pl.ds(starts[b], KB_A), :]
            for h in range(HEADS_PER_BLOCK):
                mine = (vlane < HEAD_DIM) if h == 0 else (vlane >= HEAD_DIM)
                vaug = jnp.where(mine, vblk, jnp.ones_like(vblk))
                pv = _dot(jnp.concatenate([p[h, i, b] for i in users[b]], axis=0), vaug)
                for n, i in enumerate(users[b]):
                    part = pv[n * tq:(n + 1) * tq]
                    acc[h, i] = part if (h, i) not in acc else acc[h, i] + part

        fl = lax.broadcasted_iota(jnp.int32, (tq, LANES), 1) < HEAD_DIM
        for i in range(nsub):
            num = jnp.where(fl, acc[0, i], acc[1, i])
            den = pltpu.roll(jnp.where(fl, acc[1, i], acc[0, i]), HEAD_DIM, 1)
            o_ref[0, i * tq:(i + 1) * tq, :] = (num / den).astype(o_ref.dtype)

    pl.when(qi == 0)(functools.partial(step, left // KB_A))
    pl.when(qi != 0)(functools.partial(step, 0))


def _mixer_a(q, k, v, ebias):
    b, seq, width = q.shape
    tq, nsub = TQ_A, NSUB_A
    win = LEFT_CHUNKS * CHUNK + tq
    kern = functools.partial(_mixer_a_kernel, tq=tq, nsub=nsub)
    return pl.pallas_call(
        kern,
        grid=(b, width // LANES, seq // (tq * nsub)),
        in_specs=[
            pl.BlockSpec((1, tq * nsub, LANES), lambda bi, hp, qi: (bi, qi, hp)),
            pl.BlockSpec((1, seq, LANES), lambda bi, hp, qi: (bi, 0, hp)),
            pl.BlockSpec((1, seq, LANES), lambda bi, hp, qi: (bi, 0, hp)),
            pl.BlockSpec(ebias.shape, lambda bi, hp, qi: (0, 0)),
        ],
        out_specs=pl.BlockSpec((1, tq * nsub, LANES), lambda bi, hp, qi: (bi, qi, hp)),
        out_shape=jax.ShapeDtypeStruct((b, seq, width), BF16),
        scratch_shapes=[pltpu.VMEM((HEADS_PER_BLOCK, tq, win), F32)],
        compiler_params=pltpu.CompilerParams(
            dimension_semantics=("parallel", "parallel", "arbitrary"),
            vmem_limit_bytes=VMEM_LIMIT),
        name="mixer_a",
    )(q, k, v, ebias)


def _extended_bias(rel_bias, tq):
    left = LEFT_CHUNKS * CHUNK
    top = rel_bias[:, 2 * REL_CLIP:]
    n_flat = left - REL_CLIP + 1
    n_rev = left + tq - n_flat
    lo = 2 * REL_CLIP - n_rev
    rev = rel_bias[:, lo:2 * REL_CLIP][:, ::-1]
    h = rel_bias.shape[0]
    return jnp.concatenate(
        [jnp.broadcast_to(top, (h, n_flat)), rev, jnp.broadcast_to(top, (h, tq))], axis=1)


EXP2_CLAMP = 126.0


def _softplus2(z):
    return jnp.maximum(z, jnp.log(1.0 + jnp.exp2(jnp.minimum(z, EXP2_CLAMP))) * LOG2E)


def _mixer_b_kernel(q_ref, k_ref, v_ref, negl_ref, o_ref, acc_ref, carry_ref, *, tq, tk):
    qi = pl.program_id(2)
    t0 = pl.multiple_of(qi * tq, tq)
    q = q_ref[0]
    lane = lax.broadcasted_iota(jnp.int32, q.shape, 1)
    qms = []
    for h in range(HEADS_PER_BLOCK):
        in_head = (lane >= h * HEAD_DIM) & (lane < (h + 1) * HEAD_DIM)
        qms.append(jnp.where(in_head, q, jnp.zeros_like(q)))
    negl = negl_ref[...]
    reps = tk // LANES
    n_diag = tq // tk

    acc_ref[...] = jnp.zeros_like(acc_ref)
    carry_ref[...] = jnp.zeros_like(carry_ref)

    def tile(r0, kt, vt, causal):
        for h in range(HEADS_PER_BLOCK):
            z = _dot_nt(qms[h][r0:], kt)
            sp = _softplus2(z)
            if causal is not None:
                sp = jnp.where(causal, sp, 0.0)
            negc = _dot(sp.astype(BF16), negl)
            tot = negc + jnp.concatenate([carry_ref[h, r0:, :]] * reps, axis=1)
            w = jnp.exp2(z + tot)
            if causal is not None:
                w = jnp.where(causal, w, 0.0)
            acc_ref[h, r0:, :] += _dot(w.astype(BF16), vt)
            carry_ref[h, r0:, :] = jnp.broadcast_to(tot[:, 0:1], (tq - r0, LANES))

    for j in reversed(range(n_diag)):
        r0 = j * tk
        row = lax.broadcasted_iota(jnp.int32, (tq - r0, tk), 0)
        col = lax.broadcasted_iota(jnp.int32, (tq - r0, tk), 1)
        tile(r0, k_ref[0, pl.ds(t0 + r0, tk), :], v_ref[0, pl.ds(t0 + r0, tk), :], col < row)

    n_full = qi * n_diag

    def more(state):
        it, live = state
        return (it < n_full) & (live > CARRY_EXIT)

    def body(state):
        it, _ = state
        k0 = pl.multiple_of((n_full - 1 - it) * tk, tk)
        tile(0, k_ref[0, pl.ds(k0, tk), :], v_ref[0, pl.ds(k0, tk), :], None)
        return it + 1, jnp.max(carry_ref[...])

    lax.while_loop(more, body, (jnp.int32(0), jnp.float32(0.0)))

    out = acc_ref[0]
    for h in range(1, HEADS_PER_BLOCK):
        out = jnp.where(lane >= h * HEAD_DIM, acc_ref[h], out)
    o_ref[0] = out.astype(o_ref.dtype)


def _mixer_b(q, k, v):
    b, seq, width = q.shape
    tq, tk = TQ_B, TK_B
    idx = jnp.arange(tk)
    tri = jnp.where(idx[:, None] >= idx[None, :], -1.0, 0.0).astype(BF16)
    negl = tri
    kern = functools.partial(_mixer_b_kernel, tq=tq, tk=tk)
    return pl.pallas_call(
        kern,
        grid=(b, width // LANES, seq // tq),
        in_specs=[
            pl.BlockSpec((1, tq, LANES), lambda bi, hp, qi: (bi, qi, hp)),
            pl.BlockSpec((1, seq, LANES), lambda bi, hp, qi: (bi, 0, hp)),
            pl.BlockSpec((1, seq, LANES), lambda bi, hp, qi: (bi, 0, hp)),
            pl.BlockSpec((tk, tk), lambda bi, hp, qi: (0, 0)),
        ],
        out_specs=pl.BlockSpec((1, tq, LANES), lambda bi, hp, qi: (bi, qi, hp)),
        out_shape=jax.ShapeDtypeStruct((b, seq, width), BF16),
        scratch_shapes=[pltpu.VMEM((HEADS_PER_BLOCK, tq, LANES), F32),
                        pltpu.VMEM((HEADS_PER_BLOCK, tq, LANES), F32)],
        compiler_params=pltpu.CompilerParams(
            dimension_semantics=("parallel", "parallel", "arbitrary"),
            vmem_limit_bytes=VMEM_LIMIT),
        name="mixer_b",
    )(q, k, v, negl)


def _merge_kernel(x_ref, a_ref, b_ref, ga_ref, gb_ref, wa_ref, wb_ref, wo_ref, g_ref, beta_ref,
                  o_ref):
    ya = _dot(a_ref[...], wa_ref[...])
    yb = _dot(b_ref[...], wb_ref[...])
    m = ga_ref[...].astype(F32) * ya + gb_ref[...].astype(F32) * yb
    mix = _dot(m.astype(BF16), wo_ref[...])
    r = DEEPNORM_ALPHA * x_ref[...] + mix
    o_ref[...] = _layer_norm(r, g_ref[...], beta_ref[...])


def _merge(x, a, bm, ga, gb, wa, wb, wo, g, beta):
    n = x.shape[0]
    tm = TM_MERGE
    row = lambda wd: pl.BlockSpec((tm, wd), lambda i: (i, 0))
    full = lambda arr: pl.BlockSpec(arr.shape, lambda i: (0, 0))
    return pl.pallas_call(
        _merge_kernel,
        grid=(n // tm,),
        in_specs=[row(D_MODEL), row(WIDTH_A), row(WIDTH_B), row(D_MODEL), row(D_MODEL),
                  full(wa), full(wb), full(wo), full(g), full(beta)],
        out_specs=row(D_MODEL),
        out_shape=jax.ShapeDtypeStruct((n, D_MODEL), F32),
        compiler_params=pltpu.CompilerParams(
            dimension_semantics=("parallel",), vmem_limit_bytes=VMEM_LIMIT),
        name="merge_ln",
    )(x, a, bm, ga, gb, wa, wb, wo, g, beta)


def _ffn_kernel(x_ref, w1_ref, w2_ref, g_ref, beta_ref, o_ref):
    x = x_ref[...]
    xb = x.astype(BF16)
    gate = _dot(xb, w1_ref[:, :D_FF])
    up = _dot(xb, w1_ref[:, D_FF:])
    act = gate * _sigmoid(gate) * up
    ffn = _dot(act.astype(BF16), w2_ref[...])
    r = DEEPNORM_ALPHA * x + ffn
    o_ref[...] = _layer_norm(r, g_ref[...], beta_ref[...])


def _ffn(x, w1, w2, g, beta):
    n = x.shape[0]
    tm = TM_FFN
    row = pl.BlockSpec((tm, D_MODEL), lambda i: (i, 0))
    full = lambda arr: pl.BlockSpec(arr.shape, lambda i: (0, 0))
    return pl.pallas_call(
        _ffn_kernel,
        grid=(n // tm,),
        in_specs=[row, full(w1), full(w2), full(g), full(beta)],
        out_specs=row,
        out_shape=jax.ShapeDtypeStruct((n, D_MODEL), F32),
        compiler_params=pltpu.CompilerParams(
            dimension_semantics=("parallel",), vmem_limit_bytes=VMEM_LIMIT),
        name="ffn_ln",
    )(x, w1, w2, g, beta)


def kernel(x, w_in, b_gate, rel_bias, w_proj_a, w_proj_b, w_out, ln1_g, ln1_b, w_ffn_in,
           w_ffn_out, ln2_g, ln2_b):
    b, seq, d = x.shape
    n = b * seq
    depth = w_in.shape[0]
    h = x.reshape(n, d)
    for l in range(depth):
        qa, ka, va, qb, kb, vb, ga, gb = _inproj(
            h, w_in[l].astype(BF16), b_gate[l].reshape(1, -1))
        shp = (b, seq, -1)
        att_a = _mixer_a(qa.reshape(shp), ka.reshape(shp), va.reshape(shp),
                         _extended_bias(rel_bias[l], TQ_A))
        att_b = _mixer_b(qb.reshape(shp), kb.reshape(shp), vb.reshape(shp))
        h = _merge(h, att_a.reshape(n, -1), att_b.reshape(n, -1), ga, gb,
                   w_proj_a[l].astype(BF16), w_proj_b[l].astype(BF16), w_out[l].astype(BF16),
                   ln1_g[l].reshape(1, -1), ln1_b[l].reshape(1, -1))
        h = _ffn(h, w_ffn_in[l].astype(BF16), w_ffn_out[l].astype(BF16),
                 ln2_g[l].reshape(1, -1), ln2_b[l].reshape(1, -1))
    return h.reshape(b, seq, d)
```

```python
import functools
import math

import jax
import jax.numpy as jnp
from jax import lax
from jax.experimental import pallas as pl
from jax.experimental.pallas import tpu as pltpu

D_MODEL = 1024
HEAD_DIM = 64
WIDTH_A = 512
WIDTH_B = 512
CHUNK = 64
LEFT_CHUNKS = 8
REL_CLIP = 256
D_FF = 2816
DEPTH = 2
DEEPNORM_ALPHA = (2 * DEPTH) ** 0.25
LN_EPS = 1e-5
LOG2E = 1.4426950408889634
Q_SCALE = LOG2E / math.sqrt(HEAD_DIM)
NEG_BIG = -1e30

LANES = 128
HEADS_PER_BLOCK = LANES // HEAD_DIM
VMEM_LIMIT = 56 * 1024 * 1024

TM_PROJ = 512
TM_MERGE = 512
TM_FFN = 256
TQ_A = 256
NSUB_A = 4
KB_A = 256
TQ_B = 1024
TK_B = 256
CARRY_EXIT = -200.0

BF16 = jnp.bfloat16
F32 = jnp.float32


def _dot(a, b):
    return jnp.dot(a, b, preferred_element_type=F32)


def _dot_nt(a, b):
    return lax.dot_general(a, b, (((1,), (1,)), ((), ())), preferred_element_type=F32)


def _layer_norm(r, g, b):
    mu = jnp.mean(r, axis=-1, keepdims=True)
    d = r - mu
    var = jnp.mean(d * d, axis=-1, keepdims=True)
    return d * lax.rsqrt(var + LN_EPS) * g + b


def _sigmoid(v):
    return 1.0 / (1.0 + jnp.exp(-v))


def _inproj_kernel(x_ref, w_ref, bg_ref, qa_ref, ka_ref, va_ref, qb_ref, kb_ref, vb_ref,
                   ga_ref, gb_ref):
    xb = x_ref[...].astype(BF16)

    def proj(c0, width):
        return _dot(xb, w_ref[:, c0:c0 + width])

    c = 0
    qa_ref[...] = (proj(c, WIDTH_A) * Q_SCALE).astype(BF16)
    c += WIDTH_A
    ka_ref[...] = proj(c, WIDTH_A).astype(BF16)
    c += WIDTH_A
    va_ref[...] = proj(c, WIDTH_A).astype(BF16)
    c += WIDTH_A
    qb_ref[...] = (proj(c, WIDTH_B) * Q_SCALE).astype(BF16)
    c += WIDTH_B
    kb_ref[...] = proj(c, WIDTH_B).astype(BF16)
    c += WIDTH_B
    vb_ref[...] = proj(c, WIDTH_B).astype(BF16)
    c += WIDTH_B
    ga_ref[...] = _sigmoid(proj(c, D_MODEL) + bg_ref[:, :D_MODEL]).astype(BF16)
    c += D_MODEL
    gb_ref[...] = _sigmoid(proj(c, D_MODEL) + bg_ref[:, D_MODEL:]).astype(BF16)


def _inproj(x, w, bg):
    n = x.shape[0]
    cols = w.shape[1]
    widths = [WIDTH_A] * 3 + [WIDTH_B] * 3 + [D_MODEL] * 2
    return pl.pallas_call(
        _inproj_kernel,
        grid=(n // TM_PROJ,),
        in_specs=[
            pl.BlockSpec((TM_PROJ, D_MODEL), lambda i: (i, 0)),
            pl.BlockSpec((D_MODEL, cols), lambda i: (0, 0)),
            pl.BlockSpec((1, 2 * D_MODEL), lambda i: (0, 0)),
        ],
        out_specs=[pl.BlockSpec((TM_PROJ, wd), lambda i: (i, 0)) for wd in widths],
        out_shape=[jax.ShapeDtypeStruct((n, wd), BF16) for wd in widths],
        compiler_params=pltpu.CompilerParams(
            dimension_semantics=("parallel",), vmem_limit_bytes=VMEM_LIMIT),
        name="inproj",
    )(x, w, bg)


def _mixer_a_kernel(q_ref, k_ref, v_ref, eb_ref, o_ref, bias_ref, *, tq, nsub):
    left = LEFT_CHUNKS * CHUNK
    win = left + tq
    nkb = win // KB_A
    hp = pl.program_id(1)
    qi = pl.program_id(2)

    @pl.when(qi == 0)
    def _build_bias():
        qc = lax.broadcasted_iota(jnp.int32, (tq, win), 0) // CHUNK
        kc = lax.broadcasted_iota(jnp.int32, (tq, win), 1) // CHUNK
        band = (kc >= qc) & (kc <= qc + LEFT_CHUNKS)
        for h in range(HEADS_PER_BLOCK):
            e = eb_ref[pl.ds(hp * HEADS_PER_BLOCK + h, 1), :]
            rows = jnp.broadcast_to(e, (tq, e.shape[1]))
            toep = pltpu.roll(rows, 0, 1, stride=1, stride_axis=0)[:, :win]
            bias_ref[h] = jnp.where(band, toep * LOG2E, NEG_BIG)

    assert HEADS_PER_BLOCK == 2 and tq == KB_A and tq * nsub >= left
    wb = left // KB_A + 1
    q = q_ref[0]
    lane = lax.broadcasted_iota(jnp.int32, q.shape, 1)
    vlane = lax.broadcasted_iota(jnp.int32, (KB_A, LANES), 1)
    first = lane < HEAD_DIM
    qms = [jnp.where(first, q, jnp.zeros_like(q)), jnp.where(first, jnp.zeros_like(q), q)]

    def step(first_block):
        t0 = qi * (nsub * tq)
        blocks = range(first_block, nsub + wb - 1)
        users = {b: range(max(0, b - wb + 1), min(nsub - 1, b) + 1) for b in blocks}
        starts = {b: pl.multiple_of(t0 - left + b * KB_A, KB_A) for b in blocks}

        s = {}
        for b in blocks:
            r0, r1 = users[b][0] * tq, (users[b][-1] + 1) * tq
            lhs = jnp.concatenate([qms[0][r0:r1], qms[1][r0:r1]], axis=0)
            sb = _dot_nt(lhs, k_ref[0, pl.ds(starts[b], KB_A), :])
            for h in range(HEADS_PER_BLOCK):
                for i in users[b]:
                    off = h * (r1 - r0) + (i - users[b][0]) * tq
                    kb = b - i
                    s[h, i, b] = sb[off:off + tq] + bias_ref[h, :, kb * KB_A:(kb + 1) * KB_A]

        p = {}
        for h in range(HEADS_PER_BLOCK):
            for i in range(nsub):
                mine = [s[h, i, b] for b in blocks if i in users[b]]
                m = mine[0]
                for sb in mine[1:]:
                    m = jnp.maximum(m, sb)
                for c in range(1, KB_A // LANES):
                    m = jnp.maximum(m[:, :LANES], m[:, c * LANES:(c + 1) * LANES])
                m = m.max(axis=-1, keepdims=True)
                for b in blocks:
                    if i in users[b]:
                        p[h, i, b] = jnp.exp2(s[h, i, b] - m).astype(BF16)

        acc = {}
        for b in blocks:
            vblk = v_ref[0, pl.ds(starts[b], KB_A), :]
            for h in range(HEADS_PER_BLOCK):
                mine = (vlane < HEAD_DIM) if h == 0 else (vlane >= HEAD_DIM)
                vaug = jnp.where(mine, vblk, jnp.ones_like(vblk))
                pv = _dot(jnp.concatenate([p[h, i, b] for i in users[b]], axis=0), vaug)
                for n, i in enumerate(users[b]):
                    part = pv[n * tq:(n + 1) * tq]
                    acc[h, i] = part if (h, i) not in acc else acc[h, i] + part

        fl = lax.broadcasted_iota(jnp.int32, (tq, LANES), 1) < HEAD_DIM
        for i in range(nsub):
            num = jnp.where(fl, acc[0, i], acc[1, i])
            den = pltpu.roll(jnp.where(fl, acc[1, i], acc[0, i]), HEAD_DIM, 1)
            o_ref[0, i * tq:(i + 1) * tq, :] = (num / den).astype(o_ref.dtype)

    pl.when(qi == 0)(functools.partial(step, left // KB_A))
    pl.when(qi != 0)(functools.partial(step, 0))


def _mixer_a(q, k, v, ebias):
    b, seq, width = q.shape
    tq, nsub = TQ_A, NSUB_A
    win = LEFT_CHUNKS * CHUNK + tq
    kern = functools.partial(_mixer_a_kernel, tq=tq, nsub=nsub)
    return pl.pallas_call(
        kern,
        grid=(b, width // LANES, seq // (tq * nsub)),
        in_specs=[
            pl.BlockSpec((1, tq * nsub, LANES), lambda bi, hp, qi: (bi, qi, hp)),
            pl.BlockSpec((1, seq, LANES), lambda bi, hp, qi: (bi, 0, hp)),
            pl.BlockSpec((1, seq, LANES), lambda bi, hp, qi: (bi, 0, hp)),
            pl.BlockSpec(ebias.shape, lambda bi, hp, qi: (0, 0)),
        ],
        out_specs=pl.BlockSpec((1, tq * nsub, LANES), lambda bi, hp, qi: (bi, qi, hp)),
        out_shape=jax.ShapeDtypeStruct((b, seq, width), BF16),
        scratch_shapes=[pltpu.VMEM((HEADS_PER_BLOCK, tq, win), F32)],
        compiler_params=pltpu.CompilerParams(
            dimension_semantics=("parallel", "parallel", "arbitrary"),
            vmem_limit_bytes=VMEM_LIMIT),
        name="mixer_a",
    )(q, k, v, ebias)


def _extended_bias(rel_bias, tq):
    left = LEFT_CHUNKS * CHUNK
    top = rel_bias[:, 2 * REL_CLIP:]
    n_flat = left - REL_CLIP + 1
    n_rev = left + tq - n_flat
    lo = 2 * REL_CLIP - n_rev
    rev = rel_bias[:, lo:2 * REL_CLIP][:, ::-1]
    h = rel_bias.shape[0]
    return jnp.concatenate(
        [jnp.broadcast_to(top, (h, n_flat)), rev, jnp.broadcast_to(top, (h, tq))], axis=1)


EXP2_CLAMP = 126.0


def _softplus2(z):
    return jnp.maximum(z, jnp.log(1.0 + jnp.exp2(jnp.minimum(z, EXP2_CLAMP))) * LOG2E)


def _mixer_b_kernel(q_ref, k_ref, v_ref, negl_ref, o_ref, acc_ref, carry_ref, *, tq, tk):
    qi = pl.program_id(2)
    t0 = pl.multiple_of(qi * tq, tq)
    q = q_ref[0]
    lane = lax.broadcasted_iota(jnp.int32, q.shape, 1)
    qms = []
    for h in range(HEADS_PER_BLOCK):
        in_head = (lane >= h * HEAD_DIM) & (lane < (h + 1) * HEAD_DIM)
        qms.append(jnp.where(in_head, q, jnp.zeros_like(q)))
    negl = negl_ref[...]
    reps = tk // LANES
    nb = tq // tk

    def lane_tile(c):
        return jnp.concatenate([c] * reps, axis=1)

    hb = HEADS_PER_BLOCK * tk
    zs, negcs, masks = [], [], []
    for j in range(nb):
        blocks = [i for i in (j, j + 1) if i < nb]
        lhs = jnp.concatenate([qm[i * tk:(i + 1) * tk] for i in blocks for qm in qms], axis=0)
        z = _dot_nt(lhs, k_ref[0, pl.ds(pl.multiple_of(t0 + j * tk, tk), tk), :])
        row = lax.broadcasted_iota(jnp.int32, z.shape, 0)
        col = lax.broadcasted_iota(jnp.int32, z.shape, 1)
        causal = (row >= hb) | (col < (row & (tk - 1)))
        sp = jnp.where(causal, _softplus2(z), 0.0)
        zs.append(z)
        masks.append(causal)
        negcs.append(_dot(sp.astype(BF16), negl))
    tots = [[None, None] for _ in range(nb)]
    carries = []
    for i in range(nb):
        tots[i][0] = negcs[i][:hb]
        carry = jnp.broadcast_to(tots[i][0][:, 0:1], (hb, LANES))
        if i >= 1:
            tots[i - 1][1] = negcs[i - 1][hb:] + lane_tile(carry)
            carry = jnp.broadcast_to(tots[i - 1][1][:, 0:1], (hb, LANES))
        carries.append(carry)
    pvs = []
    for j in range(nb):
        tot = jnp.concatenate([t for t in tots[j] if t is not None], axis=0)
        w = jnp.where(masks[j], jnp.exp2(zs[j] + tot), 0.0)
        pvs.append(_dot(w.astype(BF16), v_ref[0, pl.ds(pl.multiple_of(t0 + j * tk, tk), tk), :]))
    for i in range(nb):
        acc = pvs[i][:hb] if i == 0 else pvs[i][:hb] + pvs[i - 1][hb:]
        for h in range(HEADS_PER_BLOCK):
            acc_ref[h, i * tk:(i + 1) * tk, :] = acc[h * tk:(h + 1) * tk]
            carry_ref[h, i * tk:(i + 1) * tk, :] = carries[i][h * tk:(h + 1) * tk]

    def tile(r0, r1, k0):
        m = r1 - r0
        k0 = pl.multiple_of(k0, tk)
        z = _dot_nt(jnp.concatenate([qm[r0:r1] for qm in qms], axis=0), k_ref[0, pl.ds(k0, tk), :])
        negc = _dot(_softplus2(z).astype(BF16), negl)
        carry = jnp.concatenate([carry_ref[h, r0:r1, :] for h in range(HEADS_PER_BLOCK)], axis=0)
        tot = negc + lane_tile(carry)
        pv = _dot(jnp.exp2(z + tot).astype(BF16), v_ref[0, pl.ds(k0, tk), :])
        carry = jnp.broadcast_to(tot[:, 0:1], (HEADS_PER_BLOCK * m, LANES))
        for h in range(HEADS_PER_BLOCK):
            acc_ref[h, r0:r1, :] += pv[h * m:(h + 1) * m]
            carry_ref[h, r0:r1, :] = carry[h * m:(h + 1) * m]

    def peak(c):
        return jnp.max(c.reshape(c.shape[0] // 8, 8, LANES).max(axis=0))

    def peak_rows(r0, r1):
        return peak(jnp.concatenate([carry_ref[h, r0:r1, :] for h in range(HEADS_PER_BLOCK)],
                                    axis=0))

    peaks = [peak(c) for c in carries]

    def live_from(i):
        p = peaks[i]
        for other in peaks[i + 1:]:
            p = jnp.maximum(p, other)
        return p > CARRY_EXIT

    has_left = qi > 0
    pl.when(has_left)(lambda: tile(0, tk, t0 - tk))

    for j in reversed(range(nb - 2)):
        pl.when(live_from(j + 2))(functools.partial(tile, (j + 2) * tk, tq, t0 + j * tk))
    pl.when(has_left & live_from(1))(lambda: tile(tk, tq, t0 - tk))

    n_left = qi * nb - 1

    def more(state):
        it, alive = state
        return (it < n_left) & alive

    def body(state):
        it, _ = state
        tile(0, tq, t0 - (it + 2) * tk)
        return it + 1, peak_rows(0, tq) > CARRY_EXIT

    alive = live_from(1) | (peak_rows(0, tk) > CARRY_EXIT)
    lax.while_loop(more, body, (jnp.int32(0), alive))

    out = acc_ref[0]
    for h in range(1, HEADS_PER_BLOCK):
        out = jnp.where(lane >= h * HEAD_DIM, acc_ref[h], out)
    o_ref[0] = out.astype(o_ref.dtype)


def _mixer_b(q, k, v):
    b, seq, width = q.shape
    tq, tk = TQ_B, TK_B
    idx = jnp.arange(tk)
    tri = jnp.where(idx[:, None] >= idx[None, :], -1.0, 0.0).astype(BF16)
    negl = tri
    kern = functools.partial(_mixer_b_kernel, tq=tq, tk=tk)
    return pl.pallas_call(
        kern,
        grid=(b, width // LANES, seq // tq),
        in_specs=[
            pl.BlockSpec((1, tq, LANES), lambda bi, hp, qi: (bi, qi, hp)),
            pl.BlockSpec((1, seq, LANES), lambda bi, hp, qi: (bi, 0, hp)),
            pl.BlockSpec((1, seq, LANES), lambda bi, hp, qi: (bi, 0, hp)),
            pl.BlockSpec((tk, tk), lambda bi, hp, qi: (0, 0)),
        ],
        out_specs=pl.BlockSpec((1, tq, LANES), lambda bi, hp, qi: (bi, qi, hp)),
        out_shape=jax.ShapeDtypeStruct((b, seq, width), BF16),
        scratch_shapes=[pltpu.VMEM((HEADS_PER_BLOCK, tq, LANES), F32),
                        pltpu.VMEM((HEADS_PER_BLOCK, tq, LANES), F32)],
        compiler_params=pltpu.CompilerParams(
            dimension_semantics=("parallel", "parallel", "arbitrary"),
            vmem_limit_bytes=VMEM_LIMIT),
        name="mixer_b",
    )(q, k, v, negl)


def _merge_kernel(x_ref, a_ref, b_ref, ga_ref, gb_ref, wa_ref, wb_ref, wo_ref, g_ref, beta_ref,
                  o_ref):
    ya = _dot(a_ref[...], wa_ref[...])
    yb = _dot(b_ref[...], wb_ref[...])
    m = ga_ref[...].astype(F32) * ya + gb_ref[...].astype(F32) * yb
    mix = _dot(m.astype(BF16), wo_ref[...])
    r = DEEPNORM_ALPHA * x_ref[...] + mix
    o_ref[...] = _layer_norm(r, g_ref[...], beta_ref[...])


def _merge(x, a, bm, ga, gb, wa, wb, wo, g, beta):
    n = x.shape[0]
    tm = TM_MERGE
    row = lambda wd: pl.BlockSpec((tm, wd), lambda i: (i, 0))
    full = lambda arr: pl.BlockSpec(arr.shape, lambda i: (0, 0))
    return pl.pallas_call(
        _merge_kernel,
        grid=(n // tm,),
        in_specs=[row(D_MODEL), row(WIDTH_A), row(WIDTH_B), row(D_MODEL), row(D_MODEL),
                  full(wa), full(wb), full(wo), full(g), full(beta)],
        out_specs=row(D_MODEL),
        out_shape=jax.ShapeDtypeStruct((n, D_MODEL), F32),
        compiler_params=pltpu.CompilerParams(
            dimension_semantics=("parallel",), vmem_limit_bytes=VMEM_LIMIT),
        name="merge_ln",
    )(x, a, bm, ga, gb, wa, wb, wo, g, beta)


def _ffn_kernel(x_ref, w1_ref, w2_ref, g_ref, beta_ref, o_ref):
    x = x_ref[...]
    xb = x.astype(BF16)
    gate = _dot(xb, w1_ref[:, :D_FF])
    up = _dot(xb, w1_ref[:, D_FF:])
    act = gate * _sigmoid(gate) * up
    ffn = _dot(act.astype(BF16), w2_ref[...])
    r = DEEPNORM_ALPHA * x + ffn
    o_ref[...] = _layer_norm(r, g_ref[...], beta_ref[...])


def _ffn(x, w1, w2, g, beta):
    n = x.shape[0]
    tm = TM_FFN
    row = pl.BlockSpec((tm, D_MODEL), lambda i: (i, 0))
    full = lambda arr: pl.BlockSpec(arr.shape, lambda i: (0, 0))
    return pl.pallas_call(
        _ffn_kernel,
        grid=(n // tm,),
        in_specs=[row, full(w1), full(w2), full(g), full(beta)],
        out_specs=row,
        out_shape=jax.ShapeDtypeStruct((n, D_MODEL), F32),
        compiler_params=pltpu.CompilerParams(
            dimension_semantics=("parallel",), vmem_limit_bytes=VMEM_LIMIT),
        name="ffn_ln",
    )(x, w1, w2, g, beta)


def kernel(x, w_in, b_gate, rel_bias, w_proj_a, w_proj_b, w_out, ln1_g, ln1_b, w_ffn_in,
           w_ffn_out, ln2_g, ln2_b):
    b, seq, d = x.shape
    n = b * seq
    depth = w_in.shape[0]
    h = x.reshape(n, d)
    for l in range(depth):
        qa, ka, va, qb, kb, vb, ga, gb = _inproj(
            h, w_in[l].astype(BF16), b_gate[l].reshape(1, -1))
        shp = (b, seq, -1)
        att_a = _mixer_a(qa.reshape(shp), ka.reshape(shp), va.reshape(shp),
                         _extended_bias(rel_bias[l], TQ_A))
        att_b = _mixer_b(qb.reshape(shp), kb.reshape(shp), vb.reshape(shp))
        h = _merge(h, att_a.reshape(n, -1), att_b.reshape(n, -1), ga, gb,
                   w_proj_a[l].astype(BF16), w_proj_b[l].astype(BF16), w_out[l].astype(BF16),
                   ln1_g[l].reshape(1, -1), ln1_b[l].reshape(1, -1))
        h = _ffn(h, w_ffn_in[l].astype(BF16), w_ffn_out[l].astype(BF16),
                 ln2_g[l].reshape(1, -1), ln2_b[l].reshape(1, -1))
    return h.reshape(b, seq, d)
```

```python
import functools
import math

import jax
import jax.numpy as jnp
from jax import lax
from jax.experimental import pallas as pl
from jax.experimental.pallas import tpu as pltpu

D_MODEL = 1024
HEAD_DIM = 64
WIDTH_A = 512
WIDTH_B = 512
CHUNK = 64
LEFT_CHUNKS = 8
REL_CLIP = 256
D_FF = 2816
DEPTH = 2
DEEPNORM_ALPHA = (2 * DEPTH) ** 0.25
LN_EPS = 1e-5
LOG2E = 1.4426950408889634
Q_SCALE = LOG2E / math.sqrt(HEAD_DIM)
NEG_BIG = -1e30

LANES = 128
HEADS_PER_BLOCK = LANES // HEAD_DIM
VMEM_LIMIT = 56 * 1024 * 1024

TM_PROJ = 512
TM_MERGE = 1024
MERGE_SLABS = 4
TM_FFN = 512
FFN_SLABS = 2
TQ_A = 128
NSUB_A = 8
KB_A = 128
TQ_B = 1024
TK_B = 256
CARRY_EXIT = -200.0

BF16 = jnp.bfloat16
F32 = jnp.float32


def _dot(a, b):
    return jnp.dot(a, b, preferred_element_type=F32)


def _dot_nt(a, b):
    return lax.dot_general(a, b, (((1,), (1,)), ((), ())), preferred_element_type=F32)


def _layer_norm(r, g, b):
    mu = jnp.mean(r, axis=-1, keepdims=True)
    d = r - mu
    var = jnp.mean(d * d, axis=-1, keepdims=True)
    return d * lax.rsqrt(var + LN_EPS) * g + b


def _sigmoid(v):
    return 1.0 / (1.0 + jnp.exp(-v))


def _inproj_kernel(x_ref, w_ref, bg_ref, qa_ref, ka_ref, va_ref, qb_ref, kb_ref, vb_ref,
                   ga_ref, gb_ref):
    xb = x_ref[...].astype(BF16)

    def proj(c0, width):
        return _dot(xb, w_ref[:, c0:c0 + width])

    c = 0
    qa_ref[...] = (proj(c, WIDTH_A) * Q_SCALE).astype(BF16)
    c += WIDTH_A
    ka_ref[...] = proj(c, WIDTH_A).astype(BF16)
    c += WIDTH_A
    va_ref[...] = proj(c, WIDTH_A).astype(BF16)
    c += WIDTH_A
    qb_ref[...] = (proj(c, WIDTH_B) * Q_SCALE).astype(BF16)
    c += WIDTH_B
    kb_ref[...] = proj(c, WIDTH_B).astype(BF16)
    c += WIDTH_B
    vb_ref[...] = proj(c, WIDTH_B).astype(BF16)
    c += WIDTH_B
    ga_ref[...] = _sigmoid(proj(c, D_MODEL) + bg_ref[:, :D_MODEL]).astype(BF16)
    c += D_MODEL
    gb_ref[...] = _sigmoid(proj(c, D_MODEL) + bg_ref[:, D_MODEL:]).astype(BF16)


def _inproj(x, w, bg):
    n = x.shape[0]
    cols = w.shape[1]
    widths = [WIDTH_A] * 3 + [WIDTH_B] * 3 + [D_MODEL] * 2
    return pl.pallas_call(
        _inproj_kernel,
        grid=(n // TM_PROJ,),
        in_specs=[
            pl.BlockSpec((TM_PROJ, D_MODEL), lambda i: (i, 0)),
            pl.BlockSpec((D_MODEL, cols), lambda i: (0, 0)),
            pl.BlockSpec((1, 2 * D_MODEL), lambda i: (0, 0)),
        ],
        out_specs=[pl.BlockSpec((TM_PROJ, wd), lambda i: (i, 0)) for wd in widths],
        out_shape=[jax.ShapeDtypeStruct((n, wd), BF16) for wd in widths],
        compiler_params=pltpu.CompilerParams(
            dimension_semantics=("parallel",), vmem_limit_bytes=VMEM_LIMIT),
        name="inproj",
    )(x, w, bg)


def _mixer_a_kernel(q_ref, k_ref, v_ref, eb_ref, o_ref, bias_ref, *, tq, nsub):
    left = LEFT_CHUNKS * CHUNK
    win = left + tq
    nkb = win // KB_A
    hp = pl.program_id(1)
    qi = pl.program_id(2)

    @pl.when(qi == 0)
    def _build_bias():
        qc = lax.broadcasted_iota(jnp.int32, (tq, win), 0) // CHUNK
        kc = lax.broadcasted_iota(jnp.int32, (tq, win), 1) // CHUNK
        band = (kc >= qc) & (kc <= qc + LEFT_CHUNKS)
        for h in range(HEADS_PER_BLOCK):
            e = eb_ref[pl.ds(hp * HEADS_PER_BLOCK + h, 1), :]
            rows = jnp.broadcast_to(e, (tq, e.shape[1]))
            toep = pltpu.roll(rows, 0, 1, stride=1, stride_axis=0)[:, :win]
            bias_ref[h] = jnp.where(band, toep * LOG2E, NEG_BIG)

    assert HEADS_PER_BLOCK == 2 and tq == KB_A and tq * nsub >= left
    wb = left // KB_A + 1
    q = q_ref[0]
    lane = lax.broadcasted_iota(jnp.int32, q.shape, 1)
    vlane = lax.broadcasted_iota(jnp.int32, (KB_A, LANES), 1)
    first = lane < HEAD_DIM
    qms = [jnp.where(first, q, jnp.zeros_like(q)), jnp.where(first, jnp.zeros_like(q), q)]

    def step(first_block):
        t0 = qi * (nsub * tq)
        blocks = range(first_block, nsub + wb - 1)
        users = {b: range(max(0, b - wb + 1), min(nsub - 1, b) + 1) for b in blocks}
        starts = {b: pl.multiple_of(t0 - left + b * KB_A, KB_A) for b in blocks}

        s = {}
        for b in blocks:
            r0, r1 = users[b][0] * tq, (users[b][-1] + 1) * tq
            lhs = jnp.concatenate([qms[0][r0:r1], qms[1][r0:r1]], axis=0)
            sb = _dot_nt(lhs, k_ref[0, pl.ds(starts[b], KB_A), :])
            for h in range(HEADS_PER_BLOCK):
                for i in users[b]:
                    off = h * (r1 - r0) + (i - users[b][0]) * tq
                    kb = b - i
                    s[h, i, b] = sb[off:off + tq] + bias_ref[h, :, kb * KB_A:(kb + 1) * KB_A]

        p = {}
        for h in range(HEADS_PER_BLOCK):
            for i in range(nsub):
                mine = [s[h, i, b] for b in blocks if i in users[b]]
                m = mine[0]
                for sb in mine[1:]:
                    m = jnp.maximum(m, sb)
                for c in range(1, KB_A // LANES):
                    m = jnp.maximum(m[:, :LANES], m[:, c * LANES:(c + 1) * LANES])
                m = m.max(axis=-1, keepdims=True)
                for b in blocks:
                    if i in users[b]:
                        p[h, i, b] = jnp.exp2(s[h, i, b] - m).astype(BF16)

        acc = {}
        for b in blocks:
            vblk = v_ref[0, pl.ds(starts[b], KB_A), :]
            for h in range(HEADS_PER_BLOCK):
                mine = (vlane < HEAD_DIM) if h == 0 else (vlane >= HEAD_DIM)
                vaug = jnp.where(mine, vblk, jnp.ones_like(vblk))
                pv = _dot(jnp.concatenate([p[h, i, b] for i in users[b]], axis=0), vaug)
                for n, i in enumerate(users[b]):
                    part = pv[n * tq:(n + 1) * tq]
                    acc[h, i] = part if (h, i) not in acc else acc[h, i] + part

        fl = lax.broadcasted_iota(jnp.int32, (tq, LANES), 1) < HEAD_DIM
        for i in range(nsub):
            num = jnp.where(fl, acc[0, i], acc[1, i])
            den = pltpu.roll(jnp.where(fl, acc[1, i], acc[0, i]), HEAD_DIM, 1)
            o_ref[0, i * tq:(i + 1) * tq, :] = (num / den).astype(o_ref.dtype)

    pl.when(qi == 0)(functools.partial(step, left // KB_A))
    pl.when(qi != 0)(functools.partial(step, 0))


def _mixer_a(q, k, v, ebias):
    b, seq, width = q.shape
    tq, nsub = TQ_A, NSUB_A
    win = LEFT_CHUNKS * CHUNK + tq
    kern = functools.partial(_mixer_a_kernel, tq=tq, nsub=nsub)
    return pl.pallas_call(
        kern,
        grid=(b, width // LANES, seq // (tq * nsub)),
        in_specs=[
            pl.BlockSpec((1, tq * nsub, LANES), lambda bi, hp, qi: (bi, qi, hp)),
            pl.BlockSpec((1, seq, LANES), lambda bi, hp, qi: (bi, 0, hp)),
            pl.BlockSpec((1, seq, LANES), lambda bi, hp, qi: (bi, 0, hp)),
            pl.BlockSpec(ebias.shape, lambda bi, hp, qi: (0, 0)),
        ],
        out_specs=pl.BlockSpec((1, tq * nsub, LANES), lambda bi, hp, qi: (bi, qi, hp)),
        out_shape=jax.ShapeDtypeStruct((b, seq, width), BF16),
        scratch_shapes=[pltpu.VMEM((HEADS_PER_BLOCK, tq, win), F32)],
        compiler_params=pltpu.CompilerParams(
            dimension_semantics=("parallel", "parallel", "arbitrary"),
            vmem_limit_bytes=VMEM_LIMIT),
        name="mixer_a",
    )(q, k, v, ebias)


def _extended_bias(rel_bias, tq):
    left = LEFT_CHUNKS * CHUNK
    top = rel_bias[:, 2 * REL_CLIP:]
    n_flat = left - REL_CLIP + 1
    n_rev = left + tq - n_flat
    lo = 2 * REL_CLIP - n_rev
    rev = rel_bias[:, lo:2 * REL_CLIP][:, ::-1]
    h = rel_bias.shape[0]
    return jnp.concatenate(
        [jnp.broadcast_to(top, (h, n_flat)), rev, jnp.broadcast_to(top, (h, tq))], axis=1)


EXP2_CLAMP = 126.0


def _softplus2(z):
    return jnp.maximum(z, jnp.log(1.0 + jnp.exp2(jnp.minimum(z, EXP2_CLAMP))) * LOG2E)


def _mixer_b_kernel(q_ref, k_ref, v_ref, negl_ref, o_ref, acc_ref, carry_ref, *, tq, tk):
    qi = pl.program_id(2)
    t0 = pl.multiple_of(qi * tq, tq)
    q = q_ref[0]
    lane = lax.broadcasted_iota(jnp.int32, q.shape, 1)
    qms = []
    for h in range(HEADS_PER_BLOCK):
        in_head = (lane >= h * HEAD_DIM) & (lane < (h + 1) * HEAD_DIM)
        qms.append(jnp.where(in_head, q, jnp.zeros_like(q)))
    negl = negl_ref[...]
    reps = tk // LANES
    nb = tq // tk

    def lane_tile(c):
        return jnp.concatenate([c] * reps, axis=1)

    hb = HEADS_PER_BLOCK * tk
    zs, negcs, masks = [], [], []
    for j in range(nb):
        blocks = [i for i in (j, j + 1) if i < nb]
        lhs = jnp.concatenate([qm[i * tk:(i + 1) * tk] for i in blocks for qm in qms], axis=0)
        z = _dot_nt(lhs, k_ref[0, pl.ds(pl.multiple_of(t0 + j * tk, tk), tk), :])
        row = lax.broadcasted_iota(jnp.int32, z.shape, 0)
        col = lax.broadcasted_iota(jnp.int32, z.shape, 1)
        causal = (row >= hb) | (col < (row & (tk - 1)))
        sp = jnp.where(causal, _softplus2(z), 0.0)
        zs.append(z)
        masks.append(causal)
        negcs.append(_dot(sp.astype(BF16), negl))
    tots = [[None, None] for _ in range(nb)]
    carries = []
    for i in range(nb):
        tots[i][0] = negcs[i][:hb]
        carry = jnp.broadcast_to(tots[i][0][:, 0:1], (hb, LANES))
        if i >= 1:
            tots[i - 1][1] = negcs[i - 1][hb:] + lane_tile(carry)
            carry = jnp.broadcast_to(tots[i - 1][1][:, 0:1], (hb, LANES))
        carries.append(carry)
    pvs = []
    for j in range(nb):
        tot = jnp.concatenate([t for t in tots[j] if t is not None], axis=0)
        w = jnp.where(masks[j], jnp.exp2(zs[j] + tot), 0.0)
        pvs.append(_dot(w.astype(BF16), v_ref[0, pl.ds(pl.multiple_of(t0 + j * tk, tk), tk), :]))
    for i in range(nb):
        acc = pvs[i][:hb] if i == 0 else pvs[i][:hb] + pvs[i - 1][hb:]
        for h in range(HEADS_PER_BLOCK):
            acc_ref[h, i * tk:(i + 1) * tk, :] = acc[h * tk:(h + 1) * tk]
            carry_ref[h, i * tk:(i + 1) * tk, :] = carries[i][h * tk:(h + 1) * tk]

    def tile(r0, r1, k0):
        m = r1 - r0
        k0 = pl.multiple_of(k0, tk)
        z = _dot_nt(jnp.concatenate([qm[r0:r1] for qm in qms], axis=0), k_ref[0, pl.ds(k0, tk), :])
        negc = _dot(_softplus2(z).astype(BF16), negl)
        carry = jnp.concatenate([carry_ref[h, r0:r1, :] for h in range(HEADS_PER_BLOCK)], axis=0)
        tot = negc + lane_tile(carry)
        pv = _dot(jnp.exp2(z + tot).astype(BF16), v_ref[0, pl.ds(k0, tk), :])
        carry = jnp.broadcast_to(tot[:, 0:1], (HEADS_PER_BLOCK * m, LANES))
        for h in range(HEADS_PER_BLOCK):
            acc_ref[h, r0:r1, :] += pv[h * m:(h + 1) * m]
            carry_ref[h, r0:r1, :] = carry[h * m:(h + 1) * m]

    def peak(c):
        return jnp.max(c.reshape(c.shape[0] // 8, 8, LANES).max(axis=0))

    def peak_rows(r0, r1):
        return peak(jnp.concatenate([carry_ref[h, r0:r1, :] for h in range(HEADS_PER_BLOCK)],
                                    axis=0))

    peaks = [peak(c) for c in carries]

    def live_from(i):
        p = peaks[i]
        for other in peaks[i + 1:]:
            p = jnp.maximum(p, other)
        return p > CARRY_EXIT

    has_left = qi > 0
    pl.when(has_left)(lambda: tile(0, tk, t0 - tk))

    for j in reversed(range(nb - 2)):
        pl.when(live_from(j + 2))(functools.partial(tile, (j + 2) * tk, tq, t0 + j * tk))
    pl.when(has_left & live_from(1))(lambda: tile(tk, tq, t0 - tk))

    n_left = qi * nb - 1

    def more(state):
        it, alive = state
        return (it < n_left) & alive

    def body(state):
        it, _ = state
        tile(0, tq, t0 - (it + 2) * tk)
        return it + 1, peak_rows(0, tq) > CARRY_EXIT

    alive = live_from(1) | (peak_rows(0, tk) > CARRY_EXIT)
    lax.while_loop(more, body, (jnp.int32(0), alive))

    out = acc_ref[0]
    for h in range(1, HEADS_PER_BLOCK):
        out = jnp.where(lane >= h * HEAD_DIM, acc_ref[h], out)
    o_ref[0] = out.astype(o_ref.dtype)


def _mixer_b(q, k, v):
    b, seq, width = q.shape
    tq, tk = TQ_B, TK_B
    idx = jnp.arange(tk)
    tri = jnp.where(idx[:, None] >= idx[None, :], -1.0, 0.0).astype(BF16)
    negl = tri
    kern = functools.partial(_mixer_b_kernel, tq=tq, tk=tk)
    return pl.pallas_call(
        kern,
        grid=(b, width // LANES, seq // tq),
        in_specs=[
            pl.BlockSpec((1, tq, LANES), lambda bi, hp, qi: (bi, qi, hp)),
            pl.BlockSpec((1, seq, LANES), lambda bi, hp, qi: (bi, 0, hp)),
            pl.BlockSpec((1, seq, LANES), lambda bi, hp, qi: (bi, 0, hp)),
            pl.BlockSpec((tk, tk), lambda bi, hp, qi: (0, 0)),
        ],
        out_specs=pl.BlockSpec((1, tq, LANES), lambda bi, hp, qi: (bi, qi, hp)),
        out_shape=jax.ShapeDtypeStruct((b, seq, width), BF16),
        scratch_shapes=[pltpu.VMEM((HEADS_PER_BLOCK, tq, LANES), F32),
                        pltpu.VMEM((HEADS_PER_BLOCK, tq, LANES), F32)],
        compiler_params=pltpu.CompilerParams(
            dimension_semantics=("parallel", "parallel", "arbitrary"),
            vmem_limit_bytes=VMEM_LIMIT),
        name="mixer_b",
    )(q, k, v, negl)


def _merge_kernel(x_ref, a_ref, b_ref, ga_ref, gb_ref, wa_ref, wb_ref, wo_ref, g_ref, beta_ref,
                  o_ref):
    slab = x_ref.shape[0] // MERGE_SLABS
    for s in range(MERGE_SLABS):
        rows = slice(s * slab, (s + 1) * slab)
        ya = _dot(a_ref[rows, :], wa_ref[...])
        yb = _dot(b_ref[rows, :], wb_ref[...])
        m = ga_ref[rows, :].astype(F32) * ya + gb_ref[rows, :].astype(F32) * yb
        mix = _dot(m.astype(BF16), wo_ref[...])
        r = DEEPNORM_ALPHA * x_ref[rows, :] + mix
        o_ref[rows, :] = _layer_norm(r, g_ref[...], beta_ref[...])


def _merge(x, a, bm, ga, gb, wa, wb, wo, g, beta):
    n = x.shape[0]
    tm = TM_MERGE
    row = lambda wd: pl.BlockSpec((tm, wd), lambda i: (i, 0))
    full = lambda arr: pl.BlockSpec(arr.shape, lambda i: (0, 0))
    return pl.pallas_call(
        _merge_kernel,
        grid=(n // tm,),
        in_specs=[row(D_MODEL), row(WIDTH_A), row(WIDTH_B), row(D_MODEL), row(D_MODEL),
                  full(wa), full(wb), full(wo), full(g), full(beta)],
        out_specs=row(D_MODEL),
        out_shape=jax.ShapeDtypeStruct((n, D_MODEL), F32),
        compiler_params=pltpu.CompilerParams(
            dimension_semantics=("parallel",), vmem_limit_bytes=VMEM_LIMIT),
        name="merge_ln",
    )(x, a, bm, ga, gb, wa, wb, wo, g, beta)


def _ffn_kernel(x_ref, w1_ref, w2_ref, g_ref, beta_ref, o_ref):
    slab = x_ref.shape[0] // FFN_SLABS
    for s in range(FFN_SLABS):
        rows = slice(s * slab, (s + 1) * slab)
        x = x_ref[rows, :]
        xb = x.astype(BF16)
        gate = _dot(xb, w1_ref[:, :D_FF])
        up = _dot(xb, w1_ref[:, D_FF:])
        act = gate * _sigmoid(gate) * up
        ffn = _dot(act.astype(BF16), w2_ref[...])
        r = DEEPNORM_ALPHA * x + ffn
        o_ref[rows, :] = _layer_norm(r, g_ref[...], beta_ref[...])


def _ffn(x, w1, w2, g, beta):
    n = x.shape[0]
    tm = TM_FFN
    row = pl.BlockSpec((tm, D_MODEL), lambda i: (i, 0))
    full = lambda arr: pl.BlockSpec(arr.shape, lambda i: (0, 0))
    return pl.pallas_call(
        _ffn_kernel,
        grid=(n // tm,),
        in_specs=[row, full(w1), full(w2), full(g), full(beta)],
        out_specs=row,
        out_shape=jax.ShapeDtypeStruct((n, D_MODEL), F32),
        compiler_params=pltpu.CompilerParams(
            dimension_semantics=("parallel",), vmem_limit_bytes=VMEM_LIMIT),
        name="ffn_ln",
    )(x, w1, w2, g, beta)


def kernel(x, w_in, b_gate, rel_bias, w_proj_a, w_proj_b, w_out, ln1_g, ln1_b, w_ffn_in,
           w_ffn_out, ln2_g, ln2_b):
    b, seq, d = x.shape
    n = b * seq
    depth = w_in.shape[0]
    h = x.reshape(n, d)
    for l in range(depth):
        qa, ka, va, qb, kb, vb, ga, gb = _inproj(
            h, w_in[l].astype(BF16), b_gate[l].reshape(1, -1))
        shp = (b, seq, -1)
        att_a = _mixer_a(qa.reshape(shp), ka.reshape(shp), va.reshape(shp),
                         _extended_bias(rel_bias[l], TQ_A))
        att_b = _mixer_b(qb.reshape(shp), kb.reshape(shp), vb.reshape(shp))
        h = _merge(h, att_a.reshape(n, -1), att_b.reshape(n, -1), ga, gb,
                   w_proj_a[l].astype(BF16), w_proj_b[l].astype(BF16), w_out[l].astype(BF16),
                   ln1_g[l].reshape(1, -1), ln1_b[l].reshape(1, -1))
        h = _ffn(h, w_ffn_in[l].astype(BF16), w_ffn_out[l].astype(BF16),
                 ln2_g[l].reshape(1, -1), ln2_b[l].reshape(1, -1))
    return h.reshape(b, seq, d)
```

```python
import functools
import math

import jax
import jax.numpy as jnp
from jax import lax
from jax.experimental import pallas as pl
from jax.experimental.pallas import tpu as pltpu

D_MODEL = 1024
HEAD_DIM = 64
WIDTH_A = 512
WIDTH_B = 512
CHUNK = 64
LEFT_CHUNKS = 8
REL_CLIP = 256
D_FF = 2816
DEPTH = 2
DEEPNORM_ALPHA = (2 * DEPTH) ** 0.25
LN_EPS = 1e-5
LOG2E = 1.4426950408889634
Q_SCALE = LOG2E / math.sqrt(HEAD_DIM)
NEG_BIG = -1e30

LANES = 128
HEADS_PER_BLOCK = LANES // HEAD_DIM
VMEM_LIMIT = 56 * 1024 * 1024

TM_PROJ = 1024
TM_MERGE = 1024
MERGE_SLABS = 4
TM_FFN = 1024
FFN_SLABS = 4
TQ_A = 128
NSUB_A = 16
KB_A = 128
TQ_B = 1024
TK_B = 256
CARRY_EXIT = -200.0

BF16 = jnp.bfloat16
F32 = jnp.float32


def _dot(a, b):
    return jnp.dot(a, b, preferred_element_type=F32)


def _dot_nt(a, b):
    return lax.dot_general(a, b, (((1,), (1,)), ((), ())), preferred_element_type=F32)


def _layer_norm(r, g, b):
    mu = jnp.mean(r, axis=-1, keepdims=True)
    d = r - mu
    var = jnp.mean(d * d, axis=-1, keepdims=True)
    return d * lax.rsqrt(var + LN_EPS) * g + b


def _sigmoid(v):
    return 1.0 / (1.0 + jnp.exp(-v))


def _inproj_kernel(x_ref, w_ref, bg_ref, qa_ref, ka_ref, va_ref, qb_ref, kb_ref, vb_ref,
                   ga_ref, gb_ref):
    xb = x_ref[...].astype(BF16)

    def proj(c0, width):
        return _dot(xb, w_ref[:, c0:c0 + width])

    c = 3 * WIDTH_A + 3 * WIDTH_B
    ga_ref[...] = _sigmoid(proj(c, D_MODEL) + bg_ref[:, :D_MODEL]).astype(BF16)
    c += D_MODEL
    gb_ref[...] = _sigmoid(proj(c, D_MODEL) + bg_ref[:, D_MODEL:]).astype(BF16)
    c = 0
    qa_ref[...] = (proj(c, WIDTH_A) * Q_SCALE).astype(BF16)
    c += WIDTH_A
    ka_ref[...] = proj(c, WIDTH_A).astype(BF16)
    c += WIDTH_A
    va_ref[...] = proj(c, WIDTH_A).astype(BF16)
    c += WIDTH_A
    qb_ref[...] = (proj(c, WIDTH_B) * Q_SCALE).astype(BF16)
    c += WIDTH_B
    kb_ref[...] = proj(c, WIDTH_B).astype(BF16)
    c += WIDTH_B
    vb_ref[...] = proj(c, WIDTH_B).astype(BF16)


def _inproj(x, w, bg):
    n = x.shape[0]
    cols = w.shape[1]
    widths = [WIDTH_A] * 3 + [WIDTH_B] * 3 + [D_MODEL] * 2
    return pl.pallas_call(
        _inproj_kernel,
        grid=(n // TM_PROJ,),
        in_specs=[
            pl.BlockSpec((TM_PROJ, D_MODEL), lambda i: (i, 0)),
            pl.BlockSpec((D_MODEL, cols), lambda i: (0, 0), pipeline_mode=pl.Buffered(1)),
            pl.BlockSpec((1, 2 * D_MODEL), lambda i: (0, 0)),
        ],
        out_specs=[pl.BlockSpec((TM_PROJ, wd), lambda i: (i, 0)) for wd in widths],
        out_shape=[jax.ShapeDtypeStruct((n, wd), BF16) for wd in widths],
        compiler_params=pltpu.CompilerParams(
            dimension_semantics=("parallel",), vmem_limit_bytes=VMEM_LIMIT),
        name="inproj",
    )(x, w, bg)


def _mixer_a_kernel(q_ref, k_ref, v_ref, eb_ref, o_ref, bias_ref, *, tq, nsub):
    left = LEFT_CHUNKS * CHUNK
    win = left + tq
    nkb = win // KB_A
    hp = pl.program_id(1)
    qi = pl.program_id(2)

    @pl.when(qi == 0)
    def _build_bias():
        qc = lax.broadcasted_iota(jnp.int32, (tq, win), 0) // CHUNK
        kc = lax.broadcasted_iota(jnp.int32, (tq, win), 1) // CHUNK
        band = (kc >= qc) & (kc <= qc + LEFT_CHUNKS)
        for h in range(HEADS_PER_BLOCK):
            e = eb_ref[pl.ds(hp * HEADS_PER_BLOCK + h, 1), :]
            rows = jnp.broadcast_to(e, (tq, e.shape[1]))
            toep = pltpu.roll(rows, 0, 1, stride=1, stride_axis=0)[:, :win]
            bias_ref[h] = jnp.where(band, toep * LOG2E, NEG_BIG)

    assert HEADS_PER_BLOCK == 2 and tq == KB_A and tq * nsub >= left
    wb = left // KB_A + 1
    q = q_ref[0]
    lane = lax.broadcasted_iota(jnp.int32, q.shape, 1)
    vlane = lax.broadcasted_iota(jnp.int32, (KB_A, LANES), 1)
    first = lane < HEAD_DIM
    qms = [jnp.where(first, q, jnp.zeros_like(q)), jnp.where(first, jnp.zeros_like(q), q)]

    def step(first_block):
        t0 = qi * (nsub * tq)
        blocks = range(first_block, nsub + wb - 1)
        users = {b: range(max(0, b - wb + 1), min(nsub - 1, b) + 1) for b in blocks}
        starts = {b: pl.multiple_of(t0 - left + b * KB_A, KB_A) for b in blocks}

        s = {}
        for b in blocks:
            r0, r1 = users[b][0] * tq, (users[b][-1] + 1) * tq
            lhs = jnp.concatenate([qms[0][r0:r1], qms[1][r0:r1]], axis=0)
            sb = _dot_nt(lhs, k_ref[0, pl.ds(starts[b], KB_A), :])
            for h in range(HEADS_PER_BLOCK):
                for i in users[b]:
                    off = h * (r1 - r0) + (i - users[b][0]) * tq
                    kb = b - i
                    s[h, i, b] = sb[off:off + tq] + bias_ref[h, :, kb * KB_A:(kb + 1) * KB_A]

        p = {}
        for h in range(HEADS_PER_BLOCK):
            for i in range(nsub):
                mine = [s[h, i, b] for b in blocks if i in users[b]]
                m = mine[0]
                for sb in mine[1:]:
                    m = jnp.maximum(m, sb)
                for c in range(1, KB_A // LANES):
                    m = jnp.maximum(m[:, :LANES], m[:, c * LANES:(c + 1) * LANES])
                m = m.max(axis=-1, keepdims=True)
                for b in blocks:
                    if i in users[b]:
                        p[h, i, b] = jnp.exp2(s[h, i, b] - m).astype(BF16)

        acc = {}
        for b in blocks:
            vblk = v_ref[0, pl.ds(starts[b], KB_A), :]
            for h in range(HEADS_PER_BLOCK):
                mine = (vlane < HEAD_DIM) if h == 0 else (vlane >= HEAD_DIM)
                vaug = jnp.where(mine, vblk, jnp.ones_like(vblk))
                pv = _dot(jnp.concatenate([p[h, i, b] for i in users[b]], axis=0), vaug)
                for n, i in enumerate(users[b]):
                    part = pv[n * tq:(n + 1) * tq]
                    acc[h, i] = part if (h, i) not in acc else acc[h, i] + part

        fl = lax.broadcasted_iota(jnp.int32, (tq, LANES), 1) < HEAD_DIM
        for i in range(nsub):
            num = jnp.where(fl, acc[0, i], acc[1, i])
            den = pltpu.roll(jnp.where(fl, acc[1, i], acc[0, i]), HEAD_DIM, 1)
            o_ref[0, i * tq:(i + 1) * tq, :] = (num / den).astype(o_ref.dtype)

    pl.when(qi == 0)(functools.partial(step, left // KB_A))
    pl.when(qi != 0)(functools.partial(step, 0))


def _mixer_a(q, k, v, ebias):
    b, seq, width = q.shape
    tq, nsub = TQ_A, NSUB_A
    win = LEFT_CHUNKS * CHUNK + tq
    kern = functools.partial(_mixer_a_kernel, tq=tq, nsub=nsub)
    return pl.pallas_call(
        kern,
        grid=(b, width // LANES, seq // (tq * nsub)),
        in_specs=[
            pl.BlockSpec((1, tq * nsub, LANES), lambda bi, hp, qi: (bi, qi, hp)),
            pl.BlockSpec((1, seq, LANES), lambda bi, hp, qi: (bi, 0, hp)),
            pl.BlockSpec((1, seq, LANES), lambda bi, hp, qi: (bi, 0, hp)),
            pl.BlockSpec(ebias.shape, lambda bi, hp, qi: (0, 0)),
        ],
        out_specs=pl.BlockSpec((1, tq * nsub, LANES), lambda bi, hp, qi: (bi, qi, hp)),
        out_shape=jax.ShapeDtypeStruct((b, seq, width), BF16),
        scratch_shapes=[pltpu.VMEM((HEADS_PER_BLOCK, tq, win), F32)],
        compiler_params=pltpu.CompilerParams(
            dimension_semantics=("parallel", "parallel", "arbitrary"),
            vmem_limit_bytes=VMEM_LIMIT),
        name="mixer_a",
    )(q, k, v, ebias)


def _extended_bias(rel_bias, tq):
    left = LEFT_CHUNKS * CHUNK
    top = rel_bias[:, 2 * REL_CLIP:]
    n_flat = left - REL_CLIP + 1
    n_rev = left + tq - n_flat
    lo = 2 * REL_CLIP - n_rev
    rev = rel_bias[:, lo:2 * REL_CLIP][:, ::-1]
    h = rel_bias.shape[0]
    return jnp.concatenate(
        [jnp.broadcast_to(top, (h, n_flat)), rev, jnp.broadcast_to(top, (h, tq))], axis=1)


EXP2_CLAMP = 126.0


def _softplus2(z):
    return jnp.maximum(z, jnp.log(1.0 + jnp.exp2(jnp.minimum(z, EXP2_CLAMP))) * LOG2E)


def _mixer_b_kernel(q_ref, k_ref, v_ref, negl_ref, o_ref, acc_ref, carry_ref, *, tq, tk):
    qi = pl.program_id(2)
    t0 = pl.multiple_of(qi * tq, tq)
    q = q_ref[0]
    lane = lax.broadcasted_iota(jnp.int32, q.shape, 1)
    qms = []
    for h in range(HEADS_PER_BLOCK):
        in_head = (lane >= h * HEAD_DIM) & (lane < (h + 1) * HEAD_DIM)
        qms.append(jnp.where(in_head, q, jnp.zeros_like(q)))
    negl = negl_ref[...]
    reps = tk // LANES
    nb = tq // tk

    def lane_tile(c):
        return jnp.concatenate([c] * reps, axis=1)

    hb = HEADS_PER_BLOCK * tk
    zs, negcs, masks = [], [], []
    for j in range(nb):
        blocks = [i for i in (j, j + 1) if i < nb]
        lhs = jnp.concatenate([qm[i * tk:(i + 1) * tk] for i in blocks for qm in qms], axis=0)
        z = _dot_nt(lhs, k_ref[0, pl.ds(pl.multiple_of(t0 + j * tk, tk), tk), :])
        row = lax.broadcasted_iota(jnp.int32, z.shape, 0)
        col = lax.broadcasted_iota(jnp.int32, z.shape, 1)
        causal = (row >= hb) | (col < (row & (tk - 1)))
        sp = jnp.where(causal, _softplus2(z), 0.0)
        zs.append(z)
        masks.append(causal)
        negcs.append(_dot(sp.astype(BF16), negl))
    tots = [[None, None] for _ in range(nb)]
    carries = []
    for i in range(nb):
        tots[i][0] = negcs[i][:hb]
        carry = jnp.broadcast_to(tots[i][0][:, 0:1], (hb, LANES))
        if i >= 1:
            tots[i - 1][1] = negcs[i - 1][hb:] + lane_tile(carry)
            carry = jnp.broadcast_to(tots[i - 1][1][:, 0:1], (hb, LANES))
        carries.append(carry)
    pvs = []
    for j in range(nb):
        tot = jnp.concatenate([t for t in tots[j] if t is not None], axis=0)
        w = jnp.where(masks[j], jnp.exp2(zs[j] + tot), 0.0)
        pvs.append(_dot(w.astype(BF16), v_ref[0, pl.ds(pl.multiple_of(t0 + j * tk, tk), tk), :]))
    for i in range(nb):
        acc = pvs[i][:hb] if i == 0 else pvs[i][:hb] + pvs[i - 1][hb:]
        for h in range(HEADS_PER_BLOCK):
            acc_ref[h, i * tk:(i + 1) * tk, :] = acc[h * tk:(h + 1) * tk]
            carry_ref[h, i * tk:(i + 1) * tk, :] = carries[i][h * tk:(h + 1) * tk]

    def tile(r0, r1, k0):
        m = r1 - r0
        k0 = pl.multiple_of(k0, tk)
        z = _dot_nt(jnp.concatenate([qm[r0:r1] for qm in qms], axis=0), k_ref[0, pl.ds(k0, tk), :])
        negc = _dot(_softplus2(z).astype(BF16), negl)
        carry = jnp.concatenate([carry_ref[h, r0:r1, :] for h in range(HEADS_PER_BLOCK)], axis=0)
        tot = negc + lane_tile(carry)
        pv = _dot(jnp.exp2(z + tot).astype(BF16), v_ref[0, pl.ds(k0, tk), :])
        carry = jnp.broadcast_to(tot[:, 0:1], (HEADS_PER_BLOCK * m, LANES))
        for h in range(HEADS_PER_BLOCK):
            acc_ref[h, r0:r1, :] += pv[h * m:(h + 1) * m]
            carry_ref[h, r0:r1, :] = carry[h * m:(h + 1) * m]

    def peak(c):
        return jnp.max(c.reshape(c.shape[0] // 8, 8, LANES).max(axis=0))

    def peak_rows(r0, r1):
        return peak(jnp.concatenate([carry_ref[h, r0:r1, :] for h in range(HEADS_PER_BLOCK)],
                                    axis=0))

    peaks = [peak(c) for c in carries]

    def live_from(i):
        p = peaks[i]
        for other in peaks[i + 1:]:
            p = jnp.maximum(p, other)
        return p > CARRY_EXIT

    has_left = qi > 0
    pl.when(has_left)(lambda: tile(0, tk, t0 - tk))

    for j in reversed(range(nb - 2)):
        pl.when(live_from(j + 2))(functools.partial(tile, (j + 2) * tk, tq, t0 + j * tk))
    pl.when(has_left & live_from(1))(lambda: tile(tk, tq, t0 - tk))

    n_left = qi * nb - 1

    def more(state):
        it, alive = state
        return (it < n_left) & alive

    def body(state):
        it, _ = state
        tile(0, tq, t0 - (it + 2) * tk)
        return it + 1, peak_rows(0, tq) > CARRY_EXIT

    alive = live_from(1) | (peak_rows(0, tk) > CARRY_EXIT)
    lax.while_loop(more, body, (jnp.int32(0), alive))

    out = acc_ref[0]
    for h in range(1, HEADS_PER_BLOCK):
        out = jnp.where(lane >= h * HEAD_DIM, acc_ref[h], out)
    o_ref[0] = out.astype(o_ref.dtype)


def _mixer_b(q, k, v):
    b, seq, width = q.shape
    tq, tk = TQ_B, TK_B
    idx = jnp.arange(tk)
    tri = jnp.where(idx[:, None] >= idx[None, :], -1.0, 0.0).astype(BF16)
    negl = tri
    kern = functools.partial(_mixer_b_kernel, tq=tq, tk=tk)
    return pl.pallas_call(
        kern,
        grid=(b, width // LANES, seq // tq),
        in_specs=[
            pl.BlockSpec((1, tq, LANES), lambda bi, hp, qi: (bi, qi, hp)),
            pl.BlockSpec((1, seq, LANES), lambda bi, hp, qi: (bi, 0, hp)),
            pl.BlockSpec((1, seq, LANES), lambda bi, hp, qi: (bi, 0, hp)),
            pl.BlockSpec((tk, tk), lambda bi, hp, qi: (0, 0)),
        ],
        out_specs=pl.BlockSpec((1, tq, LANES), lambda bi, hp, qi: (bi, qi, hp)),
        out_shape=jax.ShapeDtypeStruct((b, seq, width), BF16),
        scratch_shapes=[pltpu.VMEM((HEADS_PER_BLOCK, tq, LANES), F32),
                        pltpu.VMEM((HEADS_PER_BLOCK, tq, LANES), F32)],
        compiler_params=pltpu.CompilerParams(
            dimension_semantics=("parallel", "parallel", "arbitrary"),
            vmem_limit_bytes=VMEM_LIMIT),
        name="mixer_b",
    )(q, k, v, negl)


def _merge_kernel(x_ref, a_ref, b_ref, ga_ref, gb_ref, wa_ref, wb_ref, wo_ref, g_ref, beta_ref,
                  o_ref):
    slab = x_ref.shape[0] // MERGE_SLABS
    for s in range(MERGE_SLABS):
        rows = slice(s * slab, (s + 1) * slab)
        ya = _dot(a_ref[rows, :], wa_ref[...])
        yb = _dot(b_ref[rows, :], wb_ref[...])
        m = ga_ref[rows, :].astype(F32) * ya + gb_ref[rows, :].astype(F32) * yb
        mix = _dot(m.astype(BF16), wo_ref[...])
        r = DEEPNORM_ALPHA * x_ref[rows, :] + mix
        o_ref[rows, :] = _layer_norm(r, g_ref[...], beta_ref[...])


def _merge(x, a, bm, ga, gb, wa, wb, wo, g, beta):
    n = x.shape[0]
    tm = TM_MERGE
    row = lambda wd: pl.BlockSpec((tm, wd), lambda i: (i, 0))
    full = lambda arr: pl.BlockSpec(arr.shape, lambda i: (0, 0))
    return pl.pallas_call(
        _merge_kernel,
        grid=(n // tm,),
        in_specs=[row(D_MODEL), row(WIDTH_A), row(WIDTH_B), row(D_MODEL), row(D_MODEL),
                  full(wa), full(wb), full(wo), full(g), full(beta)],
        out_specs=row(D_MODEL),
        out_shape=jax.ShapeDtypeStruct((n, D_MODEL), F32),
        compiler_params=pltpu.CompilerParams(
            dimension_semantics=("parallel",), vmem_limit_bytes=VMEM_LIMIT),
        name="merge_ln",
    )(x, a, bm, ga, gb, wa, wb, wo, g, beta)


def _ffn_kernel(x_ref, w1_ref, w2_ref, g_ref, beta_ref, o_ref):
    slab = x_ref.shape[0] // FFN_SLABS
    for s in range(FFN_SLABS):
        rows = slice(s * slab, (s + 1) * slab)
        x = x_ref[rows, :]
        xb = x.astype(BF16)
        gate = _dot(xb, w1_ref[:, :D_FF])
        up = _dot(xb, w1_ref[:, D_FF:])
        act = gate * _sigmoid(gate) * up
        ffn = _dot(act.astype(BF16), w2_ref[...])
        r = DEEPNORM_ALPHA * x + ffn
        o_ref[rows, :] = _layer_norm(r, g_ref[...], beta_ref[...])


def _ffn(x, w1, w2, g, beta):
    n = x.shape[0]
    tm = TM_FFN
    row = pl.BlockSpec((tm, D_MODEL), lambda i: (i, 0))
    full = lambda arr: pl.BlockSpec(arr.shape, lambda i: (0, 0), pipeline_mode=pl.Buffered(1))
    return pl.pallas_call(
        _ffn_kernel,
        grid=(n // tm,),
        in_specs=[row, full(w1), full(w2), full(g), full(beta)],
        out_specs=row,
        out_shape=jax.ShapeDtypeStruct((n, D_MODEL), F32),
        compiler_params=pltpu.CompilerParams(
            dimension_semantics=("parallel",), vmem_limit_bytes=VMEM_LIMIT),
        name="ffn_ln",
    )(x, w1, w2, g, beta)


def kernel(x, w_in, b_gate, rel_bias, w_proj_a, w_proj_b, w_out, ln1_g, ln1_b, w_ffn_in,
           w_ffn_out, ln2_g, ln2_b):
    b, seq, d = x.shape
    n = b * seq
    depth = w_in.shape[0]
    h = x.reshape(n, d)
    for l in range(depth):
        qa, ka, va, qb, kb, vb, ga, gb = _inproj(
            h, w_in[l].astype(BF16), b_gate[l].reshape(1, -1))
        shp = (b, seq, -1)
        att_a = _mixer_a(qa.reshape(shp), ka.reshape(shp), va.reshape(shp),
                         _extended_bias(rel_bias[l], TQ_A))
        att_b = _mixer_b(qb.reshape(shp), kb.reshape(shp), vb.reshape(shp))
        h = _merge(h, att_a.reshape(n, -1), att_b.reshape(n, -1), ga, gb,
                   w_proj_a[l].astype(BF16), w_proj_b[l].astype(BF16), w_out[l].astype(BF16),
                   ln1_g[l].reshape(1, -1), ln1_b[l].reshape(1, -1))
        h = _ffn(h, w_ffn_in[l].astype(BF16), w_ffn_out[l].astype(BF16),
                 ln2_g[l].reshape(1, -1), ln2_b[l].reshape(1, -1))
    return h.reshape(b, seq, d)
```

```python
import functools
import math

import jax
import jax.numpy as jnp
from jax import lax
from jax.experimental import pallas as pl
from jax.experimental.pallas import tpu as pltpu

D_MODEL = 1024
HEAD_DIM = 64
WIDTH_A = 512
WIDTH_B = 512
CHUNK = 64
LEFT_CHUNKS = 8
REL_CLIP = 256
D_FF = 2816
DEPTH = 2
DEEPNORM_ALPHA = (2 * DEPTH) ** 0.25
LN_EPS = 1e-5
LOG2E = 1.4426950408889634
Q_SCALE = LOG2E / math.sqrt(HEAD_DIM)
NEG_BIG = -1e30

LANES = 128
SUBLANES = 8
HEADS_PER_BLOCK = LANES // HEAD_DIM
VMEM_LIMIT = 56 * 1024 * 1024

TM_PROJ = 1024
TM_MERGE = 1024
MERGE_SLABS = 4
TM_FFN = 1024
FFN_SLABS = 4
TQ_A = 128
NSUB_A = 32
KB_A = 128
TQ_B = 1024
TK_B = 256
NEAR_KEYS_B = 256
CARRY_EXIT = -200.0

BF16 = jnp.bfloat16
F32 = jnp.float32


def _dot(a, b):
    return jnp.dot(a, b, preferred_element_type=F32)


def _dot_nt(a, b):
    return lax.dot_general(a, b, (((1,), (1,)), ((), ())), preferred_element_type=F32)


def _layer_norm(r, g, b):
    mu = jnp.mean(r, axis=-1, keepdims=True)
    d = r - mu
    var = jnp.mean(d * d, axis=-1, keepdims=True)
    return d * lax.rsqrt(var + LN_EPS) * g + b


def _sigmoid(v):
    return 1.0 / (1.0 + jnp.exp(-v))


def _inproj_kernel(x_ref, w_ref, bg_ref, qa_ref, ka_ref, va_ref, qb_ref, kb_ref, vb_ref,
                   ga_ref, gb_ref):
    xb = x_ref[...].astype(BF16)

    def proj(c0, width):
        return _dot(xb, w_ref[:, c0:c0 + width])

    c = 3 * WIDTH_A + 3 * WIDTH_B
    ga_ref[...] = _sigmoid(proj(c, D_MODEL) + bg_ref[:, :D_MODEL]).astype(BF16)
    c += D_MODEL
    gb_ref[...] = _sigmoid(proj(c, D_MODEL) + bg_ref[:, D_MODEL:]).astype(BF16)
    c = 0
    qa_ref[...] = (proj(c, WIDTH_A) * Q_SCALE).astype(BF16)
    c += WIDTH_A
    ka_ref[...] = proj(c, WIDTH_A).astype(BF16)
    c += WIDTH_A
    va_ref[...] = proj(c, WIDTH_A).astype(BF16)
    c += WIDTH_A
    qb_ref[...] = (proj(c, WIDTH_B) * Q_SCALE).astype(BF16)
    c += WIDTH_B
    kb_ref[...] = proj(c, WIDTH_B).astype(BF16)
    c += WIDTH_B
    vb_ref[...] = proj(c, WIDTH_B).astype(BF16)


def _inproj(x, w, bg):
    n = x.shape[0]
    cols = w.shape[1]
    widths = [WIDTH_A] * 3 + [WIDTH_B] * 3 + [D_MODEL] * 2
    return pl.pallas_call(
        _inproj_kernel,
        grid=(n // TM_PROJ,),
        in_specs=[
            pl.BlockSpec((TM_PROJ, D_MODEL), lambda i: (i, 0)),
            pl.BlockSpec((D_MODEL, cols), lambda i: (0, 0), pipeline_mode=pl.Buffered(1)),
            pl.BlockSpec((1, 2 * D_MODEL), lambda i: (0, 0)),
        ],
        out_specs=[pl.BlockSpec((TM_PROJ, wd), lambda i: (i, 0)) for wd in widths],
        out_shape=[jax.ShapeDtypeStruct((n, wd), BF16) for wd in widths],
        compiler_params=pltpu.CompilerParams(
            dimension_semantics=("parallel",), vmem_limit_bytes=VMEM_LIMIT),
        name="inproj",
    )(x, w, bg)


def _mixer_a_kernel(q_ref, k_ref, v_ref, eb_ref, o_ref, bias_ref, *, tq, nsub):
    left = LEFT_CHUNKS * CHUNK
    win = left + tq
    hp = pl.program_id(1)
    qi = pl.program_id(2)

    @pl.when(qi == 0)
    def _build_bias():
        qc = lax.broadcasted_iota(jnp.int32, (tq, win), 0) // CHUNK
        kc = lax.broadcasted_iota(jnp.int32, (tq, win), 1) // CHUNK
        band = (kc >= qc) & (kc <= qc + LEFT_CHUNKS)
        for h in range(HEADS_PER_BLOCK):
            e = eb_ref[pl.ds(hp * HEADS_PER_BLOCK + h, 1), :]
            rows = jnp.broadcast_to(e, (tq, e.shape[1]))
            toep = pltpu.roll(rows, 0, 1, stride=1, stride_axis=0)[:, :win]
            bias_ref[h] = jnp.where(band, toep * LOG2E, NEG_BIG)

    assert HEADS_PER_BLOCK == 2 and tq == KB_A and tq * nsub >= left
    wb = left // KB_A + 1
    q = q_ref[0]
    lane = lax.broadcasted_iota(jnp.int32, q.shape, 1)
    vlane = lax.broadcasted_iota(jnp.int32, (KB_A, LANES), 1)
    first = lane < HEAD_DIM
    qms = [jnp.where(first, q, jnp.zeros_like(q)), jnp.where(first, jnp.zeros_like(q), q)]

    def step(first_block):
        t0 = qi * (nsub * tq)
        blocks = range(first_block, nsub + wb - 1)
        users = {b: range(max(0, b - wb + 1), min(nsub - 1, b) + 1) for b in blocks}
        starts = {b: pl.multiple_of(t0 - left + b * KB_A, KB_A) for b in blocks}

        s = {}
        for b in blocks:
            r0, r1 = users[b][0] * tq, (users[b][-1] + 1) * tq
            lhs = jnp.concatenate([qms[0][r0:r1], qms[1][r0:r1]], axis=0)
            sb = _dot_nt(lhs, k_ref[0, pl.ds(starts[b], KB_A), :])
            for h in range(HEADS_PER_BLOCK):
                for i in users[b]:
                    off = h * (r1 - r0) + (i - users[b][0]) * tq
                    kb = b - i
                    s[h, i, b] = sb[off:off + tq] + bias_ref[h, :, kb * KB_A:(kb + 1) * KB_A]

        p = {}
        for h in range(HEADS_PER_BLOCK):
            for i in range(nsub):
                mine = [s[h, i, b] for b in blocks if i in users[b]]
                m = mine[0]
                for sb in mine[1:]:
                    m = jnp.maximum(m, sb)
                for c in range(1, KB_A // LANES):
                    m = jnp.maximum(m[:, :LANES], m[:, c * LANES:(c + 1) * LANES])
                m = m.max(axis=-1, keepdims=True)
                for b in blocks:
                    if i in users[b]:
                        p[h, i, b] = jnp.exp2(s[h, i, b] - m).astype(BF16)

        acc = {}
        for b in blocks:
            vblk = v_ref[0, pl.ds(starts[b], KB_A), :]
            for h in range(HEADS_PER_BLOCK):
                mine = (vlane < HEAD_DIM) if h == 0 else (vlane >= HEAD_DIM)
                vaug = jnp.where(mine, vblk, jnp.ones_like(vblk))
                pv = _dot(jnp.concatenate([p[h, i, b] for i in users[b]], axis=0), vaug)
                for n, i in enumerate(users[b]):
                    part = pv[n * tq:(n + 1) * tq]
                    acc[h, i] = part if (h, i) not in acc else acc[h, i] + part

        fl = lax.broadcasted_iota(jnp.int32, (tq, LANES), 1) < HEAD_DIM
        for i in range(nsub):
            num = jnp.where(fl, acc[0, i], acc[1, i])
            den = pltpu.roll(jnp.where(fl, acc[1, i], acc[0, i]), HEAD_DIM, 1)
            o_ref[0, i * tq:(i + 1) * tq, :] = (num / den).astype(o_ref.dtype)

    pl.when(qi == 0)(functools.partial(step, left // KB_A))
    pl.when(qi != 0)(functools.partial(step, 0))


def _mixer_a(q, k, v, ebias):
    b, seq, width = q.shape
    tq, nsub = TQ_A, NSUB_A
    win = LEFT_CHUNKS * CHUNK + tq
    kern = functools.partial(_mixer_a_kernel, tq=tq, nsub=nsub)
    return pl.pallas_call(
        kern,
        grid=(b, width // LANES, seq // (tq * nsub)),
        in_specs=[
            pl.BlockSpec((1, tq * nsub, LANES), lambda bi, hp, qi: (bi, qi, hp)),
            pl.BlockSpec((1, seq, LANES), lambda bi, hp, qi: (bi, 0, hp)),
            pl.BlockSpec((1, seq, LANES), lambda bi, hp, qi: (bi, 0, hp)),
            pl.BlockSpec(ebias.shape, lambda bi, hp, qi: (0, 0)),
        ],
        out_specs=pl.BlockSpec((1, tq * nsub, LANES), lambda bi, hp, qi: (bi, qi, hp)),
        out_shape=jax.ShapeDtypeStruct((b, seq, width), BF16),
        scratch_shapes=[pltpu.VMEM((HEADS_PER_BLOCK, tq, win), F32)],
        compiler_params=pltpu.CompilerParams(
            dimension_semantics=("parallel", "parallel", "arbitrary"),
            vmem_limit_bytes=VMEM_LIMIT),
        name="mixer_a",
    )(q, k, v, ebias)


def _extended_bias(rel_bias, tq):
    left = LEFT_CHUNKS * CHUNK
    top = rel_bias[:, 2 * REL_CLIP:]
    n_flat = left - REL_CLIP + 1
    n_rev = left + tq - n_flat
    lo = 2 * REL_CLIP - n_rev
    rev = rel_bias[:, lo:2 * REL_CLIP][:, ::-1]
    h = rel_bias.shape[0]
    return jnp.concatenate(
        [jnp.broadcast_to(top, (h, n_flat)), rev, jnp.broadcast_to(top, (h, tq))], axis=1)


EXP2_CLAMP = 126.0


def _softplus2(z):
    return jnp.maximum(z, jnp.log(1.0 + jnp.exp2(jnp.minimum(z, EXP2_CLAMP))) * LOG2E)


def _mixer_b_kernel(q_ref, k_ref, v_ref, negl_ref, o_ref, acc_ref, carry_ref, *, tq, tk):
    qi = pl.program_id(2)
    t0 = pl.multiple_of(qi * tq, tq)
    q = q_ref[0]
    lane = lax.broadcasted_iota(jnp.int32, q.shape, 1)
    qms = []
    for h in range(HEADS_PER_BLOCK):
        in_head = (lane >= h * HEAD_DIM) & (lane < (h + 1) * HEAD_DIM)
        qms.append(jnp.where(in_head, q, jnp.zeros_like(q)))
    negl = negl_ref[...]
    reps = tk // LANES
    nb = tq // tk
    nadj = NEAR_KEYS_B // tk
    hb = HEADS_PER_BLOCK * tk

    def lane_tile(c):
        return jnp.concatenate([c] * reps, axis=1) if reps > 1 else c

    def key_tile(ref, j):
        return ref[0, pl.ds(pl.multiple_of(t0 + j * tk, tk), tk), :]

    row = lax.broadcasted_iota(jnp.int32, (hb, tk), 0)
    col = lax.broadcasted_iota(jnp.int32, (hb, tk), 1)
    causal = col < (row & (tk - 1))

    def diag_masked(x):
        head = jnp.where(causal, x[:hb], 0.0)
        return head if x.shape[0] == hb else jnp.concatenate([head, x[hb:]], axis=0)

    zs, negcs = [], []
    for j in range(nb):
        blocks = range(j, min(j + nadj, nb - 1) + 1)
        lhs = jnp.concatenate([qm[i * tk:(i + 1) * tk] for i in blocks for qm in qms], axis=0)
        z = _dot_nt(lhs, key_tile(k_ref, j))
        zs.append(z)
        negcs.append(_dot(diag_masked(_softplus2(z)).astype(BF16), negl))
    tots = [[None] * (min(j + nadj, nb - 1) - j + 1) for j in range(nb)]
    carries = []
    for i in range(nb):
        carry = None
        for j in range(i, max(i - nadj, 0) - 1, -1):
            tot = negcs[j][(i - j) * hb:(i - j + 1) * hb]
            if carry is not None:
                tot = tot + lane_tile(carry)
            tots[j][i - j] = tot
            carry = jnp.broadcast_to(tot[:, 0:1], (hb, LANES))
        carries.append(carry)
    pvs = []
    for j in range(nb):
        w = diag_masked(jnp.exp2(zs[j] + jnp.concatenate(tots[j], axis=0)))
        pvs.append(_dot(w.astype(BF16), key_tile(v_ref, j)))
    for i in range(nb):
        acc = None
        for j in range(i, max(i - nadj, 0) - 1, -1):
            part = pvs[j][(i - j) * hb:(i - j + 1) * hb]
            acc = part if acc is None else acc + part
        for h in range(HEADS_PER_BLOCK):
            acc_ref[h, i * tk:(i + 1) * tk, :] = acc[h * tk:(h + 1) * tk]
            carry_ref[h, i * tk:(i + 1) * tk, :] = carries[i][h * tk:(h + 1) * tk]

    def peak(c):
        return jnp.max(c.reshape(c.shape[0] // SUBLANES, SUBLANES, LANES).max(axis=0))

    def tile(r0, r1, j):
        m = r1 - r0
        z = _dot_nt(jnp.concatenate([qm[r0:r1] for qm in qms], axis=0), key_tile(k_ref, j))
        negc = _dot(_softplus2(z).astype(BF16), negl)
        carry = jnp.concatenate([carry_ref[h, r0:r1, :] for h in range(HEADS_PER_BLOCK)], axis=0)
        tot = negc + lane_tile(carry)
        pv = _dot(jnp.exp2(z + tot).astype(BF16), key_tile(v_ref, j))
        carry = jnp.broadcast_to(tot[:, 0:1], (HEADS_PER_BLOCK * m, LANES))
        for h in range(HEADS_PER_BLOCK):
            acc_ref[h, r0:r1, :] += pv[h * m:(h + 1) * m]
            carry_ref[h, r0:r1, :] = carry[h * m:(h + 1) * m]
        return peak(carry)

    peaks = [peak(c) for c in carries]

    def peak_from(i):
        p = peaks[i]
        for other in peaks[i + 1:]:
            p = jnp.maximum(p, other)
        return p

    has_left = qi > 0

    def near_left():
        p = None
        for j in range(-1, -nadj - 1, -1):
            pj = tile(0, (j + nadj + 1) * tk, j)
            p = pj if p is None else jnp.maximum(p, pj)
        return p

    def no_left():
        p = peaks[0]
        for other in peaks[1:nadj]:
            p = jnp.maximum(p, other)
        return p

    near_peak = lax.cond(has_left, near_left, no_left)

    def far_tile(first, j):
        tile(first * tk, tq, j)

    for j in range(nb - 2 - nadj, -nadj - 1, -1):
        first = j + nadj + 1
        need = peak_from(first) > CARRY_EXIT
        pl.when(need if j >= 0 else has_left & need)(functools.partial(far_tile, first, j))

    n_left = qi * nb - nadj

    def more(state):
        it, alive = state
        return (it < n_left) & alive

    def body(state):
        it, _ = state
        return it + 1, tile(0, tq, -(nadj + 1) - it) > CARRY_EXIT

    alive = jnp.maximum(peak_from(nadj), near_peak) > CARRY_EXIT
    lax.while_loop(more, body, (jnp.int32(0), alive))

    out = acc_ref[0]
    for h in range(1, HEADS_PER_BLOCK):
        out = jnp.where(lane >= h * HEAD_DIM, acc_ref[h], out)
    o_ref[0] = out.astype(o_ref.dtype)


def _mixer_b(q, k, v):
    b, seq, width = q.shape
    tq, tk = TQ_B, TK_B
    idx = jnp.arange(tk)
    tri = jnp.where(idx[:, None] >= idx[None, :], -1.0, 0.0).astype(BF16)
    negl = tri
    kern = functools.partial(_mixer_b_kernel, tq=tq, tk=tk)
    return pl.pallas_call(
        kern,
        grid=(b, width // LANES, seq // tq),
        in_specs=[
            pl.BlockSpec((1, tq, LANES), lambda bi, hp, qi: (bi, qi, hp)),
            pl.BlockSpec((1, seq, LANES), lambda bi, hp, qi: (bi, 0, hp)),
            pl.BlockSpec((1, seq, LANES), lambda bi, hp, qi: (bi, 0, hp)),
            pl.BlockSpec((tk, tk), lambda bi, hp, qi: (0, 0)),
        ],
        out_specs=pl.BlockSpec((1, tq, LANES), lambda bi, hp, qi: (bi, qi, hp)),
        out_shape=jax.ShapeDtypeStruct((b, seq, width), BF16),
        scratch_shapes=[pltpu.VMEM((HEADS_PER_BLOCK, tq, LANES), F32),
                        pltpu.VMEM((HEADS_PER_BLOCK, tq, LANES), F32)],
        compiler_params=pltpu.CompilerParams(
            dimension_semantics=("parallel", "parallel", "arbitrary"),
            vmem_limit_bytes=VMEM_LIMIT),
        name="mixer_b",
    )(q, k, v, negl)


def _merge_kernel(x_ref, a_ref, b_ref, ga_ref, gb_ref, wa_ref, wb_ref, wo_ref, g_ref, beta_ref,
                  o_ref):
    slab = x_ref.shape[0] // MERGE_SLABS
    for s in range(MERGE_SLABS):
        rows = slice(s * slab, (s + 1) * slab)
        ya = _dot(a_ref[rows, :], wa_ref[...])
        yb = _dot(b_ref[rows, :], wb_ref[...])
        m = ga_ref[rows, :].astype(F32) * ya + gb_ref[rows, :].astype(F32) * yb
        mix = _dot(m.astype(BF16), wo_ref[...])
        r = DEEPNORM_ALPHA * x_ref[rows, :] + mix
        o_ref[rows, :] = _layer_norm(r, g_ref[...], beta_ref[...])


def _merge(x, a, bm, ga, gb, wa, wb, wo, g, beta):
    n = x.shape[0]
    tm = TM_MERGE
    row = lambda wd: pl.BlockSpec((tm, wd), lambda i: (i, 0))
    full = lambda arr: pl.BlockSpec(arr.shape, lambda i: (0, 0))
    return pl.pallas_call(
        _merge_kernel,
        grid=(n // tm,),
        in_specs=[row(D_MODEL), row(WIDTH_A), row(WIDTH_B), row(D_MODEL), row(D_MODEL),
                  full(wa), full(wb), full(wo), full(g), full(beta)],
        out_specs=row(D_MODEL),
        out_shape=jax.ShapeDtypeStruct((n, D_MODEL), F32),
        compiler_params=pltpu.CompilerParams(
            dimension_semantics=("parallel",), vmem_limit_bytes=VMEM_LIMIT),
        name="merge_ln",
    )(x, a, bm, ga, gb, wa, wb, wo, g, beta)


def _ffn_kernel(x_ref, w1_ref, w2_ref, g_ref, beta_ref, o_ref):
    slab = x_ref.shape[0] // FFN_SLABS
    for s in range(FFN_SLABS):
        rows = slice(s * slab, (s + 1) * slab)
        x = x_ref[rows, :]
        xb = x.astype(BF16)
        gate = _dot(xb, w1_ref[:, :D_FF])
        up = _dot(xb, w1_ref[:, D_FF:])
        act = gate * _sigmoid(gate) * up
        ffn = _dot(act.astype(BF16), w2_ref[...])
        r = DEEPNORM_ALPHA * x + ffn
        o_ref[rows, :] = _layer_norm(r, g_ref[...], beta_ref[...])


def _ffn(x, w1, w2, g, beta):
    n = x.shape[0]
    tm = TM_FFN
    row = pl.BlockSpec((tm, D_MODEL), lambda i: (i, 0))
    full = lambda arr: pl.BlockSpec(arr.shape, lambda i: (0, 0), pipeline_mode=pl.Buffered(1))
    return pl.pallas_call(
        _ffn_kernel,
        grid=(n // tm,),
        in_specs=[row, full(w1), full(w2), full(g), full(beta)],
        out_specs=row,
        out_shape=jax.ShapeDtypeStruct((n, D_MODEL), F32),
        compiler_params=pltpu.CompilerParams(
            dimension_semantics=("parallel",), vmem_limit_bytes=VMEM_LIMIT),
        name="ffn_ln",
    )(x, w1, w2, g, beta)


def kernel(x, w_in, b_gate, rel_bias, w_proj_a, w_proj_b, w_out, ln1_g, ln1_b, w_ffn_in,
           w_ffn_out, ln2_g, ln2_b):
    b, seq, d = x.shape
    n = b * seq
    depth = w_in.shape[0]
    h = x.reshape(n, d)
    for l in range(depth):
        qa, ka, va, qb, kb, vb, ga, gb = _inproj(
            h, w_in[l].astype(BF16), b_gate[l].reshape(1, -1))
        shp = (b, seq, -1)
        att_a = _mixer_a(qa.reshape(shp), ka.reshape(shp), va.reshape(shp),
                         _extended_bias(rel_bias[l], TQ_A))
        att_b = _mixer_b(qb.reshape(shp), kb.reshape(shp), vb.reshape(shp))
        h = _merge(h, att_a.reshape(n, -1), att_b.reshape(n, -1), ga, gb,
                   w_proj_a[l].astype(BF16), w_proj_b[l].astype(BF16), w_out[l].astype(BF16),
                   ln1_g[l].reshape(1, -1), ln1_b[l].reshape(1, -1))
        h = _ffn(h, w_ffn_in[l].astype(BF16), w_ffn_out[l].astype(BF16),
                 ln2_g[l].reshape(1, -1), ln2_b[l].reshape(1, -1))
    return h.reshape(b, seq, d)
```

```python
import functools
import math

import jax
import jax.numpy as jnp
from jax import lax
from jax.experimental import pallas as pl
from jax.experimental.pallas import tpu as pltpu

D_MODEL = 1024
HEAD_DIM = 64
WIDTH_A = 512
WIDTH_B = 512
CHUNK = 64
LEFT_CHUNKS = 8
REL_CLIP = 256
D_FF = 2816
DEPTH = 2
DEEPNORM_ALPHA = (2 * DEPTH) ** 0.25
LN_EPS = 1e-5
LOG2E = 1.4426950408889634
Q_SCALE = LOG2E / math.sqrt(HEAD_DIM)
NEG_BIG = -1e30

LANES = 128
SUBLANES = 8
HEADS_PER_BLOCK = LANES // HEAD_DIM
VMEM_LIMIT = 56 * 1024 * 1024

TM_PROJ = 1024
TM_MERGE = 1024
MERGE_SLABS = 4
TM_FFN = 1024
FFN_SLABS = 4
TQ_A = 128
NSUB_A = 32
KB_A = 128
TQ_B = 1024
TK_B = 256
NEAR_KEYS_B = 256
CARRY_EXIT = -200.0

BF16 = jnp.bfloat16
F32 = jnp.float32


def _dot(a, b):
    return jnp.dot(a, b, preferred_element_type=F32)


def _layer_weight(w, layer):
    return pl.BlockSpec((None,) + w.shape[1:], lambda i: (layer, 0, 0),
                        pipeline_mode=pl.Buffered(1))


def _dot_nt(a, b):
    return lax.dot_general(a, b, (((1,), (1,)), ((), ())), preferred_element_type=F32)


def _layer_norm(r, g, b):
    mu = jnp.mean(r, axis=-1, keepdims=True)
    d = r - mu
    var = jnp.mean(d * d, axis=-1, keepdims=True)
    return d * lax.rsqrt(var + LN_EPS) * g + b


def _sigmoid(v):
    return 1.0 / (1.0 + jnp.exp(-v))


def _inproj_kernel(x_ref, w_ref, bg_ref, qa_ref, ka_ref, va_ref, qb_ref, kb_ref, vb_ref,
                   ga_ref, gb_ref):
    xb = x_ref[...].astype(BF16)

    def proj(c0, width):
        return _dot(xb, w_ref[:, c0:c0 + width])

    c = 3 * WIDTH_A + 3 * WIDTH_B
    ga_ref[...] = _sigmoid(proj(c, D_MODEL) + bg_ref[:, :D_MODEL]).astype(BF16)
    c += D_MODEL
    gb_ref[...] = _sigmoid(proj(c, D_MODEL) + bg_ref[:, D_MODEL:]).astype(BF16)
    c = 0
    qa_ref[...] = (proj(c, WIDTH_A) * Q_SCALE).astype(BF16)
    c += WIDTH_A
    ka_ref[...] = proj(c, WIDTH_A).astype(BF16)
    c += WIDTH_A
    va_ref[...] = proj(c, WIDTH_A).astype(BF16)
    c += WIDTH_A
    qb_ref[...] = (proj(c, WIDTH_B) * Q_SCALE).astype(BF16)
    c += WIDTH_B
    kb_ref[...] = proj(c, WIDTH_B).astype(BF16)
    c += WIDTH_B
    vb_ref[...] = proj(c, WIDTH_B).astype(BF16)


def _inproj(x, w, layer, bg):
    n = x.shape[0]
    widths = [WIDTH_A] * 3 + [WIDTH_B] * 3 + [D_MODEL] * 2
    return pl.pallas_call(
        _inproj_kernel,
        grid=(n // TM_PROJ,),
        in_specs=[
            pl.BlockSpec((TM_PROJ, D_MODEL), lambda i: (i, 0)),
            _layer_weight(w, layer),
            pl.BlockSpec((1, 2 * D_MODEL), lambda i: (0, 0)),
        ],
        out_specs=[pl.BlockSpec((TM_PROJ, wd), lambda i: (i, 0)) for wd in widths],
        out_shape=[jax.ShapeDtypeStruct((n, wd), BF16) for wd in widths],
        compiler_params=pltpu.CompilerParams(
            dimension_semantics=("parallel",), vmem_limit_bytes=VMEM_LIMIT),
        name="inproj",
    )(x, w, bg)


def _mixer_a_kernel(q_ref, k_ref, v_ref, eb_ref, o_ref, bias_ref, *, tq, nsub):
    left = LEFT_CHUNKS * CHUNK
    win = left + tq
    hp = pl.program_id(1)
    qi = pl.program_id(2)

    @pl.when(qi == 0)
    def _build_bias():
        qc = lax.broadcasted_iota(jnp.int32, (tq, win), 0) // CHUNK
        kc = lax.broadcasted_iota(jnp.int32, (tq, win), 1) // CHUNK
        band = (kc >= qc) & (kc <= qc + LEFT_CHUNKS)
        for h in range(HEADS_PER_BLOCK):
            e = eb_ref[pl.ds(hp * HEADS_PER_BLOCK + h, 1), :]
            rows = jnp.broadcast_to(e, (tq, e.shape[1]))
            toep = pltpu.roll(rows, 0, 1, stride=1, stride_axis=0)[:, :win]
            bias_ref[h] = jnp.where(band, toep * LOG2E, NEG_BIG)

    assert HEADS_PER_BLOCK == 2 and tq == KB_A and tq * nsub >= left
    wb = left // KB_A + 1
    q = q_ref[0]
    lane = lax.broadcasted_iota(jnp.int32, q.shape, 1)
    vlane = lax.broadcasted_iota(jnp.int32, (KB_A, LANES), 1)
    first = lane < HEAD_DIM
    qms = [jnp.where(first, q, jnp.zeros_like(q)), jnp.where(first, jnp.zeros_like(q), q)]

    def step(first_block):
        t0 = qi * (nsub * tq)
        blocks = range(first_block, nsub + wb - 1)
        users = {b: range(max(0, b - wb + 1), min(nsub - 1, b) + 1) for b in blocks}
        starts = {b: pl.multiple_of(t0 - left + b * KB_A, KB_A) for b in blocks}

        s = {}
        for b in blocks:
            r0, r1 = users[b][0] * tq, (users[b][-1] + 1) * tq
            lhs = jnp.concatenate([qms[0][r0:r1], qms[1][r0:r1]], axis=0)
            sb = _dot_nt(lhs, k_ref[0, pl.ds(starts[b], KB_A), :])
            for h in range(HEADS_PER_BLOCK):
                for i in users[b]:
                    off = h * (r1 - r0) + (i - users[b][0]) * tq
                    kb = b - i
                    s[h, i, b] = sb[off:off + tq] + bias_ref[h, :, kb * KB_A:(kb + 1) * KB_A]

        p = {}
        for h in range(HEADS_PER_BLOCK):
            for i in range(nsub):
                mine = [s[h, i, b] for b in blocks if i in users[b]]
                m = mine[0]
                for sb in mine[1:]:
                    m = jnp.maximum(m, sb)
                for c in range(1, KB_A // LANES):
                    m = jnp.maximum(m[:, :LANES], m[:, c * LANES:(c + 1) * LANES])
                m = m.max(axis=-1, keepdims=True)
                for b in blocks:
                    if i in users[b]:
                        p[h, i, b] = jnp.exp2(s[h, i, b] - m).astype(BF16)

        acc = {}
        for b in blocks:
            vblk = v_ref[0, pl.ds(starts[b], KB_A), :]
            for h in range(HEADS_PER_BLOCK):
                mine = (vlane < HEAD_DIM) if h == 0 else (vlane >= HEAD_DIM)
                vaug = jnp.where(mine, vblk, jnp.ones_like(vblk))
                pv = _dot(jnp.concatenate([p[h, i, b] for i in users[b]], axis=0), vaug)
                for n, i in enumerate(users[b]):
                    part = pv[n * tq:(n + 1) * tq]
                    acc[h, i] = part if (h, i) not in acc else acc[h, i] + part

        fl = lax.broadcasted_iota(jnp.int32, (tq, LANES), 1) < HEAD_DIM
        for i in range(nsub):
            num = jnp.where(fl, acc[0, i], acc[1, i])
            den = pltpu.roll(jnp.where(fl, acc[1, i], acc[0, i]), HEAD_DIM, 1)
            o_ref[0, i * tq:(i + 1) * tq, :] = (num / den).astype(o_ref.dtype)

    pl.when(qi == 0)(functools.partial(step, left // KB_A))
    pl.when(qi != 0)(functools.partial(step, 0))


def _mixer_a(q, k, v, ebias):
    b, seq, width = q.shape
    tq, nsub = TQ_A, NSUB_A
    win = LEFT_CHUNKS * CHUNK + tq
    kern = functools.partial(_mixer_a_kernel, tq=tq, nsub=nsub)
    return pl.pallas_call(
        kern,
        grid=(b, width // LANES, seq // (tq * nsub)),
        in_specs=[
            pl.BlockSpec((1, tq * nsub, LANES), lambda bi, hp, qi: (bi, qi, hp)),
            pl.BlockSpec((1, seq, LANES), lambda bi, hp, qi: (bi, 0, hp)),
            pl.BlockSpec((1, seq, LANES), lambda bi, hp, qi: (bi, 0, hp)),
            pl.BlockSpec(ebias.shape, lambda bi, hp, qi: (0, 0)),
        ],
        out_specs=pl.BlockSpec((1, tq * nsub, LANES), lambda bi, hp, qi: (bi, qi, hp)),
        out_shape=jax.ShapeDtypeStruct((b, seq, width), BF16),
        scratch_shapes=[pltpu.VMEM((HEADS_PER_BLOCK, tq, win), F32)],
        compiler_params=pltpu.CompilerParams(
            dimension_semantics=("parallel", "parallel", "arbitrary"),
            vmem_limit_bytes=VMEM_LIMIT),
        name="mixer_a",
    )(q, k, v, ebias)


def _extended_bias(rel_bias, tq):
    left = LEFT_CHUNKS * CHUNK
    top = rel_bias[:, 2 * REL_CLIP:]
    n_flat = left - REL_CLIP + 1
    n_rev = left + tq - n_flat
    lo = 2 * REL_CLIP - n_rev
    rev = rel_bias[:, lo:2 * REL_CLIP][:, ::-1]
    h = rel_bias.shape[0]
    return jnp.concatenate(
        [jnp.broadcast_to(top, (h, n_flat)), rev, jnp.broadcast_to(top, (h, tq))], axis=1)


EXP2_CLAMP = 126.0


def _softplus2(z):
    return jnp.maximum(z, jnp.log(1.0 + jnp.exp2(jnp.minimum(z, EXP2_CLAMP))) * LOG2E)


def _mixer_b_kernel(q_ref, k_ref, v_ref, negl_ref, o_ref, acc_ref, carry_ref, *, tq, tk):
    qi = pl.program_id(2)
    t0 = pl.multiple_of(qi * tq, tq)
    q = q_ref[0]
    lane = lax.broadcasted_iota(jnp.int32, q.shape, 1)
    qms = []
    for h in range(HEADS_PER_BLOCK):
        in_head = (lane >= h * HEAD_DIM) & (lane < (h + 1) * HEAD_DIM)
        qms.append(jnp.where(in_head, q, jnp.zeros_like(q)))
    negl = negl_ref[...]
    reps = tk // LANES
    nb = tq // tk
    nadj = NEAR_KEYS_B // tk
    hb = HEADS_PER_BLOCK * tk

    def lane_tile(c):
        return jnp.concatenate([c] * reps, axis=1) if reps > 1 else c

    def key_tile(ref, j):
        return ref[0, pl.ds(pl.multiple_of(t0 + j * tk, tk), tk), :]

    row = lax.broadcasted_iota(jnp.int32, (hb, tk), 0)
    col = lax.broadcasted_iota(jnp.int32, (hb, tk), 1)
    causal = col < (row & (tk - 1))

    def diag_masked(x):
        head = jnp.where(causal, x[:hb], 0.0)
        return head if x.shape[0] == hb else jnp.concatenate([head, x[hb:]], axis=0)

    zs, negcs = [], []
    for j in range(nb):
        blocks = range(j, min(j + nadj, nb - 1) + 1)
        lhs = jnp.concatenate([qm[i * tk:(i + 1) * tk] for i in blocks for qm in qms], axis=0)
        z = _dot_nt(lhs, key_tile(k_ref, j))
        zs.append(z)
        negcs.append(_dot(diag_masked(_softplus2(z)).astype(BF16), negl))
    tots = [[None] * (min(j + nadj, nb - 1) - j + 1) for j in range(nb)]
    carries = []
    for i in range(nb):
        carry = None
        for j in range(i, max(i - nadj, 0) - 1, -1):
            tot = negcs[j][(i - j) * hb:(i - j + 1) * hb]
            if carry is not None:
                tot = tot + lane_tile(carry)
            tots[j][i - j] = tot
            carry = jnp.broadcast_to(tot[:, 0:1], (hb, LANES))
        carries.append(carry)
    pvs = []
    for j in range(nb):
        w = diag_masked(jnp.exp2(zs[j] + jnp.concatenate(tots[j], axis=0)))
        pvs.append(_dot(w.astype(BF16), key_tile(v_ref, j)))
    for i in range(nb):
        acc = None
        for j in range(i, max(i - nadj, 0) - 1, -1):
            part = pvs[j][(i - j) * hb:(i - j + 1) * hb]
            acc = part if acc is None else acc + part
        for h in range(HEADS_PER_BLOCK):
            acc_ref[h, i * tk:(i + 1) * tk, :] = acc[h * tk:(h + 1) * tk]
            carry_ref[h, i * tk:(i + 1) * tk, :] = carries[i][h * tk:(h + 1) * tk]

    def peak(c):
        return jnp.max(c.reshape(c.shape[0] // SUBLANES, SUBLANES, LANES).max(axis=0))

    def tile(r0, r1, j):
        m = r1 - r0
        z = _dot_nt(jnp.concatenate([qm[r0:r1] for qm in qms], axis=0), key_tile(k_ref, j))
        negc = _dot(_softplus2(z).astype(BF16), negl)
        carry = jnp.concatenate([carry_ref[h, r0:r1, :] for h in range(HEADS_PER_BLOCK)], axis=0)
        tot = negc + lane_tile(carry)
        pv = _dot(jnp.exp2(z + tot).astype(BF16), key_tile(v_ref, j))
        carry = jnp.broadcast_to(tot[:, 0:1], (HEADS_PER_BLOCK * m, LANES))
        for h in range(HEADS_PER_BLOCK):
            acc_ref[h, r0:r1, :] += pv[h * m:(h + 1) * m]
            carry_ref[h, r0:r1, :] = carry[h * m:(h + 1) * m]
        return peak(carry)

    peaks = [peak(c) for c in carries]

    def peak_from(i):
        p = peaks[i]
        for other in peaks[i + 1:]:
            p = jnp.maximum(p, other)
        return p

    has_left = qi > 0

    def near_left():
        p = None
        for j in range(-1, -nadj - 1, -1):
            pj = tile(0, (j + nadj + 1) * tk, j)
            p = pj if p is None else jnp.maximum(p, pj)
        return p

    def no_left():
        p = peaks[0]
        for other in peaks[1:nadj]:
            p = jnp.maximum(p, other)
        return p

    near_peak = lax.cond(has_left, near_left, no_left)

    def far_tile(first, j):
        tile(first * tk, tq, j)

    for j in range(nb - 2 - nadj, -nadj - 1, -1):
        first = j + nadj + 1
        need = peak_from(first) > CARRY_EXIT
        pl.when(need if j >= 0 else has_left & need)(functools.partial(far_tile, first, j))

    n_left = qi * nb - nadj

    def more(state):
        it, alive = state
        return (it < n_left) & alive

    def body(state):
        it, _ = state
        return it + 1, tile(0, tq, -(nadj + 1) - it) > CARRY_EXIT

    alive = jnp.maximum(peak_from(nadj), near_peak) > CARRY_EXIT
    lax.while_loop(more, body, (jnp.int32(0), alive))

    out = acc_ref[0]
    for h in range(1, HEADS_PER_BLOCK):
        out = jnp.where(lane >= h * HEAD_DIM, acc_ref[h], out)
    o_ref[0] = out.astype(o_ref.dtype)


def _mixer_b(q, k, v):
    b, seq, width = q.shape
    tq, tk = TQ_B, TK_B
    idx = jnp.arange(tk)
    tri = jnp.where(idx[:, None] >= idx[None, :], -1.0, 0.0).astype(BF16)
    negl = tri
    kern = functools.partial(_mixer_b_kernel, tq=tq, tk=tk)
    return pl.pallas_call(
        kern,
        grid=(b, width // LANES, seq // tq),
        in_specs=[
            pl.BlockSpec((1, tq, LANES), lambda bi, hp, qi: (bi, qi, hp)),
            pl.BlockSpec((1, seq, LANES), lambda bi, hp, qi: (bi, 0, hp)),
            pl.BlockSpec((1, seq, LANES), lambda bi, hp, qi: (bi, 0, hp)),
            pl.BlockSpec((tk, tk), lambda bi, hp, qi: (0, 0)),
        ],
        out_specs=pl.BlockSpec((1, tq, LANES), lambda bi, hp, qi: (bi, qi, hp)),
        out_shape=jax.ShapeDtypeStruct((b, seq, width), BF16),
        scratch_shapes=[pltpu.VMEM((HEADS_PER_BLOCK, tq, LANES), F32),
                        pltpu.VMEM((HEADS_PER_BLOCK, tq, LANES), F32)],
        compiler_params=pltpu.CompilerParams(
            dimension_semantics=("parallel", "parallel", "arbitrary"),
            vmem_limit_bytes=VMEM_LIMIT),
        name="mixer_b",
    )(q, k, v, negl)


def _merge_kernel(x_ref, a_ref, b_ref, ga_ref, gb_ref, wa_ref, wb_ref, wo_ref, g_ref, beta_ref,
                  o_ref):
    slab = x_ref.shape[0] // MERGE_SLABS
    for s in range(MERGE_SLABS):
        rows = slice(s * slab, (s + 1) * slab)
        ya = _dot(a_ref[rows, :], wa_ref[...])
        yb = _dot(b_ref[rows, :], wb_ref[...])
        m = ga_ref[rows, :].astype(F32) * ya + gb_ref[rows, :].astype(F32) * yb
        mix = _dot(m.astype(BF16), wo_ref[...])
        r = DEEPNORM_ALPHA * x_ref[rows, :] + mix
        o_ref[rows, :] = _layer_norm(r, g_ref[...], beta_ref[...])


def _merge(x, a, bm, ga, gb, wa, wb, wo, layer, g, beta):
    n = x.shape[0]
    tm = TM_MERGE
    row = lambda wd: pl.BlockSpec((tm, wd), lambda i: (i, 0))
    full = lambda arr: pl.BlockSpec(arr.shape, lambda i: (0, 0))
    return pl.pallas_call(
        _merge_kernel,
        grid=(n // tm,),
        in_specs=[row(D_MODEL), row(WIDTH_A), row(WIDTH_B), row(D_MODEL), row(D_MODEL),
                  _layer_weight(wa, layer), _layer_weight(wb, layer), _layer_weight(wo, layer),
                  full(g), full(beta)],
        out_specs=row(D_MODEL),
        out_shape=jax.ShapeDtypeStruct((n, D_MODEL), F32),
        compiler_params=pltpu.CompilerParams(
            dimension_semantics=("parallel",), vmem_limit_bytes=VMEM_LIMIT),
        name="merge_ln",
    )(x, a, bm, ga, gb, wa, wb, wo, g, beta)


def _ffn_kernel(x_ref, w1_ref, w2_ref, g_ref, beta_ref, o_ref):
    slab = x_ref.shape[0] // FFN_SLABS
    for s in range(FFN_SLABS):
        rows = slice(s * slab, (s + 1) * slab)
        x = x_ref[rows, :]
        xb = x.astype(BF16)
        gate = _dot(xb, w1_ref[:, :D_FF])
        up = _dot(xb, w1_ref[:, D_FF:])
        act = gate * _sigmoid(gate) * up
        ffn = _dot(act.astype(BF16), w2_ref[...])
        r = DEEPNORM_ALPHA * x + ffn
        o_ref[rows, :] = _layer_norm(r, g_ref[...], beta_ref[...])


def _ffn(x, w1, w2, layer, g, beta):
    n = x.shape[0]
    tm = TM_FFN
    row = pl.BlockSpec((tm, D_MODEL), lambda i: (i, 0))
    full = lambda arr: pl.BlockSpec(arr.shape, lambda i: (0, 0))
    return pl.pallas_call(
        _ffn_kernel,
        grid=(n // tm,),
        in_specs=[row, _layer_weight(w1, layer), _layer_weight(w2, layer), full(g), full(beta)],
        out_specs=row,
        out_shape=jax.ShapeDtypeStruct((n, D_MODEL), F32),
        compiler_params=pltpu.CompilerParams(
            dimension_semantics=("parallel",), vmem_limit_bytes=VMEM_LIMIT),
        name="ffn_ln",
    )(x, w1, w2, g, beta)


def kernel(x, w_in, b_gate, rel_bias, w_proj_a, w_proj_b, w_out, ln1_g, ln1_b, w_ffn_in,
           w_ffn_out, ln2_g, ln2_b):
    b, seq, d = x.shape
    n = b * seq
    depth = w_in.shape[0]
    h = x.reshape(n, d)
    w_in, w_proj_a, w_proj_b, w_out, w_ffn_in, w_ffn_out = (
        w.astype(BF16) for w in (w_in, w_proj_a, w_proj_b, w_out, w_ffn_in, w_ffn_out))
    for l in range(depth):
        qa, ka, va, qb, kb, vb, ga, gb = _inproj(h, w_in, l, b_gate[l].reshape(1, -1))
        shp = (b, seq, -1)
        att_a = _mixer_a(qa.reshape(shp), ka.reshape(shp), va.reshape(shp),
                         _extended_bias(rel_bias[l], TQ_A))
        att_b = _mixer_b(qb.reshape(shp), kb.reshape(shp), vb.reshape(shp))
        h = _merge(h, att_a.reshape(n, -1), att_b.reshape(n, -1), ga, gb,
                   w_proj_a, w_proj_b, w_out, l,
                   ln1_g[l].reshape(1, -1), ln1_b[l].reshape(1, -1))
        h = _ffn(h, w_ffn_in, w_ffn_out, l, ln2_g[l].reshape(1, -1), ln2_b[l].reshape(1, -1))
    return h.reshape(b, seq, d)
```

```python
import functools
import math

import jax
import jax.numpy as jnp
from jax import lax
from jax.experimental import pallas as pl
from jax.experimental.pallas import tpu as pltpu

D_MODEL = 1024
HEAD_DIM = 64
WIDTH_A = 512
WIDTH_B = 512
CHUNK = 64
LEFT_CHUNKS = 8
REL_CLIP = 256
D_FF = 2816
DEPTH = 2
DEEPNORM_ALPHA = (2 * DEPTH) ** 0.25
LN_EPS = 1e-5
LOG2E = 1.4426950408889634
Q_SCALE = LOG2E / math.sqrt(HEAD_DIM)
NEG_BIG = -1e30

LANES = 128
SUBLANES = 8
HEADS_PER_BLOCK = LANES // HEAD_DIM
VMEM_LIMIT = 56 * 1024 * 1024

TM_PROJ = 1024
TM_MERGE = 1024
MERGE_SLABS = 4
TM_FFN = 1024
FFN_SLABS = 4
TQ_A = 128
NSUB_A = 32
KB_A = 128
TQ_B = 2048
TK_B = 256
NEAR_KEYS_B = 256
CARRY_EXIT = -200.0

BF16 = jnp.bfloat16
F32 = jnp.float32


def _dot(a, b):
    return jnp.dot(a, b, preferred_element_type=F32)


def _layer_weight(w, layer):
    return pl.BlockSpec((None,) + w.shape[1:], lambda i: (layer, 0, 0),
                        pipeline_mode=pl.Buffered(1))


def _dot_nt(a, b):
    return lax.dot_general(a, b, (((1,), (1,)), ((), ())), preferred_element_type=F32)


def _layer_norm(r, g, b):
    mu = jnp.mean(r, axis=-1, keepdims=True)
    d = r - mu
    var = jnp.mean(d * d, axis=-1, keepdims=True)
    return d * lax.rsqrt(var + LN_EPS) * g + b


def _sigmoid(v):
    return 1.0 / (1.0 + jnp.exp(-v))


def _inproj_kernel(x_ref, w_ref, bg_ref, qa_ref, ka_ref, va_ref, qb_ref, kb_ref, vb_ref,
                   ga_ref, gb_ref):
    xb = x_ref[...].astype(BF16)

    def proj(c0, width):
        return _dot(xb, w_ref[:, c0:c0 + width])

    c = 3 * WIDTH_A + 3 * WIDTH_B
    ga_ref[...] = _sigmoid(proj(c, D_MODEL) + bg_ref[:, :D_MODEL]).astype(BF16)
    c += D_MODEL
    gb_ref[...] = _sigmoid(proj(c, D_MODEL) + bg_ref[:, D_MODEL:]).astype(BF16)
    c = 0
    qa_ref[...] = (proj(c, WIDTH_A) * Q_SCALE).astype(BF16)
    c += WIDTH_A
    ka_ref[...] = proj(c, WIDTH_A).astype(BF16)
    c += WIDTH_A
    va_ref[...] = proj(c, WIDTH_A).astype(BF16)
    c += WIDTH_A
    qb_ref[...] = (proj(c, WIDTH_B) * Q_SCALE).astype(BF16)
    c += WIDTH_B
    kb_ref[...] = proj(c, WIDTH_B).astype(BF16)
    c += WIDTH_B
    vb_ref[...] = proj(c, WIDTH_B).astype(BF16)


def _inproj(x, w, layer, bg):
    n = x.shape[0]
    widths = [WIDTH_A] * 3 + [WIDTH_B] * 3 + [D_MODEL] * 2
    return pl.pallas_call(
        _inproj_kernel,
        grid=(n // TM_PROJ,),
        in_specs=[
            pl.BlockSpec((TM_PROJ, D_MODEL), lambda i: (i, 0)),
            _layer_weight(w, layer),
            pl.BlockSpec((1, 2 * D_MODEL), lambda i: (0, 0)),
        ],
        out_specs=[pl.BlockSpec((TM_PROJ, wd), lambda i: (i, 0)) for wd in widths],
        out_shape=[jax.ShapeDtypeStruct((n, wd), BF16) for wd in widths],
        compiler_params=pltpu.CompilerParams(
            dimension_semantics=("parallel",), vmem_limit_bytes=VMEM_LIMIT),
        name="inproj",
    )(x, w, bg)


def _mixer_a_kernel(q_ref, k_ref, v_ref, eb_ref, o_ref, bias_ref, *, tq, nsub):
    left = LEFT_CHUNKS * CHUNK
    win = left + tq
    hp = pl.program_id(1)
    qi = pl.program_id(2)

    @pl.when(qi == 0)
    def _build_bias():
        qc = lax.broadcasted_iota(jnp.int32, (tq, win), 0) // CHUNK
        kc = lax.broadcasted_iota(jnp.int32, (tq, win), 1) // CHUNK
        band = (kc >= qc) & (kc <= qc + LEFT_CHUNKS)
        for h in range(HEADS_PER_BLOCK):
            e = eb_ref[pl.ds(hp * HEADS_PER_BLOCK + h, 1), :]
            rows = jnp.broadcast_to(e, (tq, e.shape[1]))
            toep = pltpu.roll(rows, 0, 1, stride=1, stride_axis=0)[:, :win]
            bias_ref[h] = jnp.where(band, toep * LOG2E, NEG_BIG)

    assert HEADS_PER_BLOCK == 2 and tq == KB_A and tq * nsub >= left
    wb = left // KB_A + 1
    q = q_ref[0]
    lane = lax.broadcasted_iota(jnp.int32, q.shape, 1)
    vlane = lax.broadcasted_iota(jnp.int32, (KB_A, LANES), 1)
    first = lane < HEAD_DIM
    qms = [jnp.where(first, q, jnp.zeros_like(q)), jnp.where(first, jnp.zeros_like(q), q)]

    def step(first_block):
        t0 = qi * (nsub * tq)
        blocks = range(first_block, nsub + wb - 1)
        users = {b: range(max(0, b - wb + 1), min(nsub - 1, b) + 1) for b in blocks}
        starts = {b: pl.multiple_of(t0 - left + b * KB_A, KB_A) for b in blocks}

        s = {}
        for b in blocks:
            r0, r1 = users[b][0] * tq, (users[b][-1] + 1) * tq
            lhs = jnp.concatenate([qms[0][r0:r1], qms[1][r0:r1]], axis=0)
            sb = _dot_nt(lhs, k_ref[0, pl.ds(starts[b], KB_A), :])
            for h in range(HEADS_PER_BLOCK):
                for i in users[b]:
                    off = h * (r1 - r0) + (i - users[b][0]) * tq
                    kb = b - i
                    s[h, i, b] = sb[off:off + tq] + bias_ref[h, :, kb * KB_A:(kb + 1) * KB_A]

        p = {}
        for h in range(HEADS_PER_BLOCK):
            for i in range(nsub):
                mine = [s[h, i, b] for b in blocks if i in users[b]]
                m = mine[0]
                for sb in mine[1:]:
                    m = jnp.maximum(m, sb)
                for c in range(1, KB_A // LANES):
                    m = jnp.maximum(m[:, :LANES], m[:, c * LANES:(c + 1) * LANES])
                m = m.max(axis=-1, keepdims=True)
                for b in blocks:
                    if i in users[b]:
                        p[h, i, b] = jnp.exp2(s[h, i, b] - m).astype(BF16)

        acc = {}
        for b in blocks:
            vblk = v_ref[0, pl.ds(starts[b], KB_A), :]
            for h in range(HEADS_PER_BLOCK):
                mine = (vlane < HEAD_DIM) if h == 0 else (vlane >= HEAD_DIM)
                vaug = jnp.where(mine, vblk, jnp.ones_like(vblk))
                pv = _dot(jnp.concatenate([p[h, i, b] for i in users[b]], axis=0), vaug)
                for n, i in enumerate(users[b]):
                    part = pv[n * tq:(n + 1) * tq]
                    acc[h, i] = part if (h, i) not in acc else acc[h, i] + part

        fl = lax.broadcasted_iota(jnp.int32, (tq, LANES), 1) < HEAD_DIM
        for i in range(nsub):
            num = jnp.where(fl, acc[0, i], acc[1, i])
            den = pltpu.roll(jnp.where(fl, acc[1, i], acc[0, i]), HEAD_DIM, 1)
            o_ref[0, i * tq:(i + 1) * tq, :] = (num / den).astype(o_ref.dtype)

    pl.when(qi == 0)(functools.partial(step, left // KB_A))
    pl.when(qi != 0)(functools.partial(step, 0))


def _mixer_a(q, k, v, ebias):
    b, seq, width = q.shape
    tq, nsub = TQ_A, NSUB_A
    win = LEFT_CHUNKS * CHUNK + tq
    kern = functools.partial(_mixer_a_kernel, tq=tq, nsub=nsub)
    return pl.pallas_call(
        kern,
        grid=(b, width // LANES, seq // (tq * nsub)),
        in_specs=[
            pl.BlockSpec((1, tq * nsub, LANES), lambda bi, hp, qi: (bi, qi, hp)),
            pl.BlockSpec((1, seq, LANES), lambda bi, hp, qi: (bi, 0, hp)),
            pl.BlockSpec((1, seq, LANES), lambda bi, hp, qi: (bi, 0, hp)),
            pl.BlockSpec(ebias.shape, lambda bi, hp, qi: (0, 0)),
        ],
        out_specs=pl.BlockSpec((1, tq * nsub, LANES), lambda bi, hp, qi: (bi, qi, hp)),
        out_shape=jax.ShapeDtypeStruct((b, seq, width), BF16),
        scratch_shapes=[pltpu.VMEM((HEADS_PER_BLOCK, tq, win), F32)],
        compiler_params=pltpu.CompilerParams(
            dimension_semantics=("parallel", "parallel", "arbitrary"),
            vmem_limit_bytes=VMEM_LIMIT),
        name="mixer_a",
    )(q, k, v, ebias)


def _extended_bias(rel_bias, tq):
    left = LEFT_CHUNKS * CHUNK
    top = rel_bias[:, 2 * REL_CLIP:]
    n_flat = left - REL_CLIP + 1
    n_rev = left + tq - n_flat
    lo = 2 * REL_CLIP - n_rev
    rev = rel_bias[:, lo:2 * REL_CLIP][:, ::-1]
    h = rel_bias.shape[0]
    return jnp.concatenate(
        [jnp.broadcast_to(top, (h, n_flat)), rev, jnp.broadcast_to(top, (h, tq))], axis=1)


EXP2_CLAMP = 126.0


def _softplus2(z):
    return jnp.maximum(z, jnp.log(1.0 + jnp.exp2(jnp.minimum(z, EXP2_CLAMP))) * LOG2E)


def _mixer_b_kernel(q_ref, k_ref, v_ref, negl_ref, o_ref, acc_ref, carry_ref, *, tq, tk):
    qi = pl.program_id(2)
    t0 = pl.multiple_of(qi * tq, tq)
    q = q_ref[0]
    lane = lax.broadcasted_iota(jnp.int32, q.shape, 1)
    qms = []
    for h in range(HEADS_PER_BLOCK):
        in_head = (lane >= h * HEAD_DIM) & (lane < (h + 1) * HEAD_DIM)
        qms.append(jnp.where(in_head, q, jnp.zeros_like(q)))
    negl = negl_ref[...]
    reps = tk // LANES
    nb = tq // tk
    nadj = NEAR_KEYS_B // tk
    hb = HEADS_PER_BLOCK * tk

    def lane_tile(c):
        return jnp.concatenate([c] * reps, axis=1) if reps > 1 else c

    def key_tile(ref, j):
        return ref[0, pl.ds(pl.multiple_of(t0 + j * tk, tk), tk), :]

    row = lax.broadcasted_iota(jnp.int32, (hb, tk), 0)
    col = lax.broadcasted_iota(jnp.int32, (hb, tk), 1)
    causal = col < (row & (tk - 1))

    def diag_masked(x):
        head = jnp.where(causal, x[:hb], 0.0)
        return head if x.shape[0] == hb else jnp.concatenate([head, x[hb:]], axis=0)

    zs, negcs = [], []
    for j in range(nb):
        blocks = range(j, min(j + nadj, nb - 1) + 1)
        lhs = jnp.concatenate([qm[i * tk:(i + 1) * tk] for i in blocks for qm in qms], axis=0)
        z = _dot_nt(lhs, key_tile(k_ref, j))
        zs.append(z)
        negcs.append(_dot(diag_masked(_softplus2(z)).astype(BF16), negl))
    tots = [[None] * (min(j + nadj, nb - 1) - j + 1) for j in range(nb)]
    carries = []
    for i in range(nb):
        carry = None
        for j in range(i, max(i - nadj, 0) - 1, -1):
            tot = negcs[j][(i - j) * hb:(i - j + 1) * hb]
            if carry is not None:
                tot = tot + lane_tile(carry)
            tots[j][i - j] = tot
            carry = jnp.broadcast_to(tot[:, 0:1], (hb, LANES))
        carries.append(carry)
    pvs = []
    for j in range(nb):
        w = diag_masked(jnp.exp2(zs[j] + jnp.concatenate(tots[j], axis=0)))
        pvs.append(_dot(w.astype(BF16), key_tile(v_ref, j)))
    for i in range(nb):
        acc = None
        for j in range(i, max(i - nadj, 0) - 1, -1):
            part = pvs[j][(i - j) * hb:(i - j + 1) * hb]
            acc = part if acc is None else acc + part
        for h in range(HEADS_PER_BLOCK):
            acc_ref[h, i * tk:(i + 1) * tk, :] = acc[h * tk:(h + 1) * tk]
            carry_ref[h, i * tk:(i + 1) * tk, :] = carries[i][h * tk:(h + 1) * tk]

    def peak(c):
        return jnp.max(c.reshape(c.shape[0] // SUBLANES, SUBLANES, LANES).max(axis=0))

    def tile(r0, r1, j):
        m = r1 - r0
        z = _dot_nt(jnp.concatenate([qm[r0:r1] for qm in qms], axis=0), key_tile(k_ref, j))
        negc = _dot(_softplus2(z).astype(BF16), negl)
        carry = jnp.concatenate([carry_ref[h, r0:r1, :] for h in range(HEADS_PER_BLOCK)], axis=0)
        tot = negc + lane_tile(carry)
        pv = _dot(jnp.exp2(z + tot).astype(BF16), key_tile(v_ref, j))
        carry = jnp.broadcast_to(tot[:, 0:1], (HEADS_PER_BLOCK * m, LANES))
        for h in range(HEADS_PER_BLOCK):
            acc_ref[h, r0:r1, :] += pv[h * m:(h + 1) * m]
            carry_ref[h, r0:r1, :] = carry[h * m:(h + 1) * m]
        return peak(carry)

    peaks = [peak(c) for c in carries]

    def peak_from(i):
        p = peaks[i]
        for other in peaks[i + 1:]:
            p = jnp.maximum(p, other)
        return p

    has_left = qi > 0

    def near_left():
        p = None
        for j in range(-1, -nadj - 1, -1):
            pj = tile(0, (j + nadj + 1) * tk, j)
            p = pj if p is None else jnp.maximum(p, pj)
        return p

    def no_left():
        p = peaks[0]
        for other in peaks[1:nadj]:
            p = jnp.maximum(p, other)
        return p

    near_peak = lax.cond(has_left, near_left, no_left)

    def far_tile(first, j):
        tile(first * tk, tq, j)

    for j in range(nb - 2 - nadj, -nadj - 1, -1):
        first = j + nadj + 1
        need = peak_from(first) > CARRY_EXIT
        pl.when(need if j >= 0 else has_left & need)(functools.partial(far_tile, first, j))

    n_left = qi * nb - nadj

    def more(state):
        it, alive = state
        return (it < n_left) & alive

    def body(state):
        it, _ = state
        return it + 1, tile(0, tq, -(nadj + 1) - it) > CARRY_EXIT

    alive = jnp.maximum(peak_from(nadj), near_peak) > CARRY_EXIT
    lax.while_loop(more, body, (jnp.int32(0), alive))

    out = acc_ref[0]
    for h in range(1, HEADS_PER_BLOCK):
        out = jnp.where(lane >= h * HEAD_DIM, acc_ref[h], out)
    o_ref[0] = out.astype(o_ref.dtype)


def _mixer_b(q, k, v):
    b, seq, width = q.shape
    tq, tk = TQ_B, TK_B
    idx = jnp.arange(tk)
    tri = jnp.where(idx[:, None] >= idx[None, :], -1.0, 0.0).astype(BF16)
    negl = tri
    kern = functools.partial(_mixer_b_kernel, tq=tq, tk=tk)
    return pl.pallas_call(
        kern,
        grid=(b, width // LANES, seq // tq),
        in_specs=[
            pl.BlockSpec((1, tq, LANES), lambda bi, hp, qi: (bi, qi, hp)),
            pl.BlockSpec((1, seq, LANES), lambda bi, hp, qi: (bi, 0, hp)),
            pl.BlockSpec((1, seq, LANES), lambda bi, hp, qi: (bi, 0, hp)),
            pl.BlockSpec((tk, tk), lambda bi, hp, qi: (0, 0)),
        ],
        out_specs=pl.BlockSpec((1, tq, LANES), lambda bi, hp, qi: (bi, qi, hp)),
        out_shape=jax.ShapeDtypeStruct((b, seq, width), BF16),
        scratch_shapes=[pltpu.VMEM((HEADS_PER_BLOCK, tq, LANES), F32),
                        pltpu.VMEM((HEADS_PER_BLOCK, tq, LANES), F32)],
        compiler_params=pltpu.CompilerParams(
            dimension_semantics=("parallel", "parallel", "arbitrary"),
            vmem_limit_bytes=VMEM_LIMIT),
        name="mixer_b",
    )(q, k, v, negl)


def _merge_kernel(x_ref, a_ref, b_ref, ga_ref, gb_ref, wa_ref, wb_ref, wo_ref, g_ref, beta_ref,
                  o_ref):
    slab = x_ref.shape[0] // MERGE_SLABS
    for s in range(MERGE_SLABS):
        rows = slice(s * slab, (s + 1) * slab)
        ya = _dot(a_ref[rows, :], wa_ref[...])
        yb = _dot(b_ref[rows, :], wb_ref[...])
        m = ga_ref[rows, :].astype(F32) * ya + gb_ref[rows, :].astype(F32) * yb
        mix = _dot(m.astype(BF16), wo_ref[...])
        r = DEEPNORM_ALPHA * x_ref[rows, :] + mix
        o_ref[rows, :] = _layer_norm(r, g_ref[...], beta_ref[...])


def _merge(x, a, bm, ga, gb, wa, wb, wo, layer, g, beta):
    n = x.shape[0]
    tm = TM_MERGE
    row = lambda wd: pl.BlockSpec((tm, wd), lambda i: (i, 0))
    full = lambda arr: pl.BlockSpec(arr.shape, lambda i: (0, 0))
    return pl.pallas_call(
        _merge_kernel,
        grid=(n // tm,),
        in_specs=[row(D_MODEL), row(WIDTH_A), row(WIDTH_B), row(D_MODEL), row(D_MODEL),
                  _layer_weight(wa, layer), _layer_weight(wb, layer), _layer_weight(wo, layer),
                  full(g), full(beta)],
        out_specs=row(D_MODEL),
        out_shape=jax.ShapeDtypeStruct((n, D_MODEL), F32),
        compiler_params=pltpu.CompilerParams(
            dimension_semantics=("parallel",), vmem_limit_bytes=VMEM_LIMIT),
        name="merge_ln",
    )(x, a, bm, ga, gb, wa, wb, wo, g, beta)


def _ffn_kernel(x_ref, w1_ref, w2_ref, g_ref, beta_ref, o_ref):
    slab = x_ref.shape[0] // FFN_SLABS
    for s in range(FFN_SLABS):
        rows = slice(s * slab, (s + 1) * slab)
        x = x_ref[rows, :]
        xb = x.astype(BF16)
        gate = _dot(xb, w1_ref[:, :D_FF])
        up = _dot(xb, w1_ref[:, D_FF:])
        act = gate * _sigmoid(gate) * up
        ffn = _dot(act.astype(BF16), w2_ref[...])
        r = DEEPNORM_ALPHA * x + ffn
        o_ref[rows, :] = _layer_norm(r, g_ref[...], beta_ref[...])


def _ffn(x, w1, w2, layer, g, beta):
    n = x.shape[0]
    tm = TM_FFN
    row = pl.BlockSpec((tm, D_MODEL), lambda i: (i, 0))
    full = lambda arr: pl.BlockSpec(arr.shape, lambda i: (0, 0))
    return pl.pallas_call(
        _ffn_kernel,
        grid=(n // tm,),
        in_specs=[row, _layer_weight(w1, layer), _layer_weight(w2, layer), full(g), full(beta)],
        out_specs=row,
        out_shape=jax.ShapeDtypeStruct((n, D_MODEL), F32),
        compiler_params=pltpu.CompilerParams(
            dimension_semantics=("parallel",), vmem_limit_bytes=VMEM_LIMIT),
        name="ffn_ln",
    )(x, w1, w2, g, beta)


def kernel(x, w_in, b_gate, rel_bias, w_proj_a, w_proj_b, w_out, ln1_g, ln1_b, w_ffn_in,
           w_ffn_out, ln2_g, ln2_b):
    b, seq, d = x.shape
    n = b * seq
    depth = w_in.shape[0]
    h = x.reshape(n, d)
    w_in, w_proj_a, w_proj_b, w_out, w_ffn_in, w_ffn_out = (
        w.astype(BF16) for w in (w_in, w_proj_a, w_proj_b, w_out, w_ffn_in, w_ffn_out))
    for l in range(depth):
        qa, ka, va, qb, kb, vb, ga, gb = _inproj(h, w_in, l, b_gate[l].reshape(1, -1))
        shp = (b, seq, -1)
        att_a = _mixer_a(qa.reshape(shp), ka.reshape(shp), va.reshape(shp),
                         _extended_bias(rel_bias[l], TQ_A))
        att_b = _mixer_b(qb.reshape(shp), kb.reshape(shp), vb.reshape(shp))
        h = _merge(h, att_a.reshape(n, -1), att_b.reshape(n, -1), ga, gb,
                   w_proj_a, w_proj_b, w_out, l,
                   ln1_g[l].reshape(1, -1), ln1_b[l].reshape(1, -1))
        h = _ffn(h, w_ffn_in, w_ffn_out, l, ln2_g[l].reshape(1, -1), ln2_b[l].reshape(1, -1))
    return h.reshape(b, seq, d)
```

```python
import functools
import math

import jax
import jax.numpy as jnp
from jax import lax
from jax.experimental import pallas as pl
from jax.experimental.pallas import tpu as pltpu

D_MODEL = 1024
HEAD_DIM = 64
WIDTH_A = 512
WIDTH_B = 512
CHUNK = 64
LEFT_CHUNKS = 8
REL_CLIP = 256
D_FF = 2816
DEPTH = 2
DEEPNORM_ALPHA = (2 * DEPTH) ** 0.25
LN_EPS = 1e-5
LOG2E = 1.4426950408889634
Q_SCALE = LOG2E / math.sqrt(HEAD_DIM)
NEG_BIG = -1e30

LANES = 128
SUBLANES = 8
HEADS_PER_BLOCK = LANES // HEAD_DIM
VMEM_LIMIT = 56 * 1024 * 1024

TM_PROJ = 1024
TM_MERGE = 1024
MERGE_SLABS = 4
TM_FFN = 1024
FFN_SLABS = 4
TQ_A = 128
NSUB_A = 32
KB_A = 128
TQ_B = 4096
TK_B = 256
NEAR_KEYS_B = 256
CARRY_EXIT = -200.0

BF16 = jnp.bfloat16
F32 = jnp.float32


def _dot(a, b):
    return jnp.dot(a, b, preferred_element_type=F32)


def _layer_weight(w, layer):
    return pl.BlockSpec((None,) + w.shape[1:], lambda i: (layer, 0, 0),
                        pipeline_mode=pl.Buffered(1))


def _dot_nt(a, b):
    return lax.dot_general(a, b, (((1,), (1,)), ((), ())), preferred_element_type=F32)


def _layer_norm(r, g, b):
    mu = jnp.mean(r, axis=-1, keepdims=True)
    d = r - mu
    var = jnp.mean(d * d, axis=-1, keepdims=True)
    return d * lax.rsqrt(var + LN_EPS) * g + b


def _sigmoid(v):
    return 1.0 / (1.0 + jnp.exp(-v))


def _inproj_kernel(x_ref, w_ref, bg_ref, qa_ref, ka_ref, va_ref, qb_ref, kb_ref, vb_ref,
                   ga_ref, gb_ref):
    xb = x_ref[...].astype(BF16)

    def proj(c0, width):
        return _dot(xb, w_ref[:, c0:c0 + width])

    c = 3 * WIDTH_A + 3 * WIDTH_B
    ga_ref[...] = _sigmoid(proj(c, D_MODEL) + bg_ref[:, :D_MODEL]).astype(BF16)
    c += D_MODEL
    gb_ref[...] = _sigmoid(proj(c, D_MODEL) + bg_ref[:, D_MODEL:]).astype(BF16)
    c = 0
    qa_ref[...] = (proj(c, WIDTH_A) * Q_SCALE).astype(BF16)
    c += WIDTH_A
    ka_ref[...] = proj(c, WIDTH_A).astype(BF16)
    c += WIDTH_A
    va_ref[...] = proj(c, WIDTH_A).astype(BF16)
    c += WIDTH_A
    qb_ref[...] = (proj(c, WIDTH_B) * Q_SCALE).astype(BF16)
    c += WIDTH_B
    kb_ref[...] = proj(c, WIDTH_B).astype(BF16)
    c += WIDTH_B
    vb_ref[...] = proj(c, WIDTH_B).astype(BF16)


def _inproj(x, w, layer, bg):
    n = x.shape[0]
    widths = [WIDTH_A] * 3 + [WIDTH_B] * 3 + [D_MODEL] * 2
    return pl.pallas_call(
        _inproj_kernel,
        grid=(n // TM_PROJ,),
        in_specs=[
            pl.BlockSpec((TM_PROJ, D_MODEL), lambda i: (i, 0)),
            _layer_weight(w, layer),
            pl.BlockSpec((1, 2 * D_MODEL), lambda i: (0, 0)),
        ],
        out_specs=[pl.BlockSpec((TM_PROJ, wd), lambda i: (i, 0)) for wd in widths],
        out_shape=[jax.ShapeDtypeStruct((n, wd), BF16) for wd in widths],
        compiler_params=pltpu.CompilerParams(
            dimension_semantics=("parallel",), vmem_limit_bytes=VMEM_LIMIT),
        name="inproj",
    )(x, w, bg)


def _mixer_a_kernel(q_ref, k_ref, v_ref, eb_ref, o_ref, bias_ref, *, tq, nsub):
    left = LEFT_CHUNKS * CHUNK
    win = left + tq
    hp = pl.program_id(1)
    qi = pl.program_id(2)

    @pl.when(qi == 0)
    def _build_bias():
        qc = lax.broadcasted_iota(jnp.int32, (tq, win), 0) // CHUNK
        kc = lax.broadcasted_iota(jnp.int32, (tq, win), 1) // CHUNK
        band = (kc >= qc) & (kc <= qc + LEFT_CHUNKS)
        for h in range(HEADS_PER_BLOCK):
            e = eb_ref[pl.ds(hp * HEADS_PER_BLOCK + h, 1), :]
            rows = jnp.broadcast_to(e, (tq, e.shape[1]))
            toep = pltpu.roll(rows, 0, 1, stride=1, stride_axis=0)[:, :win]
            bias_ref[h] = jnp.where(band, toep * LOG2E, NEG_BIG)

    assert HEADS_PER_BLOCK == 2 and tq == KB_A and tq * nsub >= left
    wb = left // KB_A + 1
    q = q_ref[0]
    lane = lax.broadcasted_iota(jnp.int32, q.shape, 1)
    vlane = lax.broadcasted_iota(jnp.int32, (KB_A, LANES), 1)
    first = lane < HEAD_DIM
    qms = [jnp.where(first, q, jnp.zeros_like(q)), jnp.where(first, jnp.zeros_like(q), q)]

    def step(first_block):
        t0 = qi * (nsub * tq)
        blocks = range(first_block, nsub + wb - 1)
        users = {b: range(max(0, b - wb + 1), min(nsub - 1, b) + 1) for b in blocks}
        starts = {b: pl.multiple_of(t0 - left + b * KB_A, KB_A) for b in blocks}

        s = {}
        for b in blocks:
            r0, r1 = users[b][0] * tq, (users[b][-1] + 1) * tq
            lhs = jnp.concatenate([qms[0][r0:r1], qms[1][r0:r1]], axis=0)
            sb = _dot_nt(lhs, k_ref[0, pl.ds(starts[b], KB_A), :])
            for h in range(HEADS_PER_BLOCK):
                for i in users[b]:
                    off = h * (r1 - r0) + (i - users[b][0]) * tq
                    kb = b - i
                    s[h, i, b] = sb[off:off + tq] + bias_ref[h, :, kb * KB_A:(kb + 1) * KB_A]

        p = {}
        for h in range(HEADS_PER_BLOCK):
            for i in range(nsub):
                mine = [s[h, i, b] for b in blocks if i in users[b]]
                m = mine[0]
                for sb in mine[1:]:
                    m = jnp.maximum(m, sb)
                for c in range(1, KB_A // LANES):
                    m = jnp.maximum(m[:, :LANES], m[:, c * LANES:(c + 1) * LANES])
                m = m.max(axis=-1, keepdims=True)
                for b in blocks:
                    if i in users[b]:
                        p[h, i, b] = jnp.exp2(s[h, i, b] - m).astype(BF16)

        acc = {}
        for b in blocks:
            vblk = v_ref[0, pl.ds(starts[b], KB_A), :]
            for h in range(HEADS_PER_BLOCK):
                mine = (vlane < HEAD_DIM) if h == 0 else (vlane >= HEAD_DIM)
                vaug = jnp.where(mine, vblk, jnp.ones_like(vblk))
                pv = _dot(jnp.concatenate([p[h, i, b] for i in users[b]], axis=0), vaug)
                for n, i in enumerate(users[b]):
                    part = pv[n * tq:(n + 1) * tq]
                    acc[h, i] = part if (h, i) not in acc else acc[h, i] + part

        fl = lax.broadcasted_iota(jnp.int32, (tq, LANES), 1) < HEAD_DIM
        for i in range(nsub):
            num = jnp.where(fl, acc[0, i], acc[1, i])
            den = pltpu.roll(jnp.where(fl, acc[1, i], acc[0, i]), HEAD_DIM, 1)
            o_ref[0, i * tq:(i + 1) * tq, :] = (num / den).astype(o_ref.dtype)

    pl.when(qi == 0)(functools.partial(step, left // KB_A))
    pl.when(qi != 0)(functools.partial(step, 0))


def _mixer_a(q, k, v, ebias):
    b, seq, width = q.shape
    tq, nsub = TQ_A, NSUB_A
    win = LEFT_CHUNKS * CHUNK + tq
    kern = functools.partial(_mixer_a_kernel, tq=tq, nsub=nsub)
    return pl.pallas_call(
        kern,
        grid=(b, width // LANES, seq // (tq * nsub)),
        in_specs=[
            pl.BlockSpec((1, tq * nsub, LANES), lambda bi, hp, qi: (bi, qi, hp)),
            pl.BlockSpec((1, seq, LANES), lambda bi, hp, qi: (bi, 0, hp)),
            pl.BlockSpec((1, seq, LANES), lambda bi, hp, qi: (bi, 0, hp)),
            pl.BlockSpec(ebias.shape, lambda bi, hp, qi: (0, 0)),
        ],
        out_specs=pl.BlockSpec((1, tq * nsub, LANES), lambda bi, hp, qi: (bi, qi, hp)),
        out_shape=jax.ShapeDtypeStruct((b, seq, width), BF16),
        scratch_shapes=[pltpu.VMEM((HEADS_PER_BLOCK, tq, win), F32)],
        compiler_params=pltpu.CompilerParams(
            dimension_semantics=("parallel", "parallel", "arbitrary"),
            vmem_limit_bytes=VMEM_LIMIT),
        name="mixer_a",
    )(q, k, v, ebias)


def _extended_bias(rel_bias, tq):
    left = LEFT_CHUNKS * CHUNK
    top = rel_bias[:, 2 * REL_CLIP:]
    n_flat = left - REL_CLIP + 1
    n_rev = left + tq - n_flat
    lo = 2 * REL_CLIP - n_rev
    rev = rel_bias[:, lo:2 * REL_CLIP][:, ::-1]
    h = rel_bias.shape[0]
    return jnp.concatenate(
        [jnp.broadcast_to(top, (h, n_flat)), rev, jnp.broadcast_to(top, (h, tq))], axis=1)


EXP2_CLAMP = 126.0


def _softplus2(z):
    return jnp.maximum(z, jnp.log(1.0 + jnp.exp2(jnp.minimum(z, EXP2_CLAMP))) * LOG2E)


def _mixer_b_kernel(q_ref, k_ref, v_ref, negl_ref, o_ref, acc_ref, carry_ref, *, tq, tk):
    qi = pl.program_id(2)
    t0 = pl.multiple_of(qi * tq, tq)
    q = q_ref[0]
    lane = lax.broadcasted_iota(jnp.int32, q.shape, 1)
    qms = []
    for h in range(HEADS_PER_BLOCK):
        in_head = (lane >= h * HEAD_DIM) & (lane < (h + 1) * HEAD_DIM)
        qms.append(jnp.where(in_head, q, jnp.zeros_like(q)))
    negl = negl_ref[...]
    reps = tk // LANES
    nb = tq // tk
    nadj = NEAR_KEYS_B // tk
    hb = HEADS_PER_BLOCK * tk

    def lane_tile(c):
        return jnp.concatenate([c] * reps, axis=1) if reps > 1 else c

    def key_tile(ref, j):
        return ref[0, pl.ds(pl.multiple_of(t0 + j * tk, tk), tk), :]

    row = lax.broadcasted_iota(jnp.int32, (hb, tk), 0)
    col = lax.broadcasted_iota(jnp.int32, (hb, tk), 1)
    causal = col < (row & (tk - 1))

    def diag_masked(x):
        head = jnp.where(causal, x[:hb], 0.0)
        return head if x.shape[0] == hb else jnp.concatenate([head, x[hb:]], axis=0)

    zs, negcs = [], []
    for j in range(nb):
        blocks = range(j, min(j + nadj, nb - 1) + 1)
        lhs = jnp.concatenate([qm[i * tk:(i + 1) * tk] for i in blocks for qm in qms], axis=0)
        z = _dot_nt(lhs, key_tile(k_ref, j))
        zs.append(z)
        negcs.append(_dot(diag_masked(_softplus2(z)).astype(BF16), negl))
    tots = [[None] * (min(j + nadj, nb - 1) - j + 1) for j in range(nb)]
    carries = []
    for i in range(nb):
        carry = None
        for j in range(i, max(i - nadj, 0) - 1, -1):
            tot = negcs[j][(i - j) * hb:(i - j + 1) * hb]
            if carry is not None:
                tot = tot + lane_tile(carry)
            tots[j][i - j] = tot
            carry = jnp.broadcast_to(tot[:, 0:1], (hb, LANES))
        carries.append(carry)
    pvs = []
    for j in range(nb):
        w = diag_masked(jnp.exp2(zs[j] + jnp.concatenate(tots[j], axis=0)))
        pvs.append(_dot(w.astype(BF16), key_tile(v_ref, j)))
    for i in range(nb):
        acc = None
        for j in range(i, max(i - nadj, 0) - 1, -1):
            part = pvs[j][(i - j) * hb:(i - j + 1) * hb]
            acc = part if acc is None else acc + part
        for h in range(HEADS_PER_BLOCK):
            acc_ref[h, i * tk:(i + 1) * tk, :] = acc[h * tk:(h + 1) * tk]
            carry_ref[h, i * tk:(i + 1) * tk, :] = carries[i][h * tk:(h + 1) * tk]

    def peak(c):
        return jnp.max(c.reshape(c.shape[0] // SUBLANES, SUBLANES, LANES).max(axis=0))

    def tile(r0, r1, j):
        m = r1 - r0
        z = _dot_nt(jnp.concatenate([qm[r0:r1] for qm in qms], axis=0), key_tile(k_ref, j))
        negc = _dot(_softplus2(z).astype(BF16), negl)
        carry = jnp.concatenate([carry_ref[h, r0:r1, :] for h in range(HEADS_PER_BLOCK)], axis=0)
        tot = negc + lane_tile(carry)
        pv = _dot(jnp.exp2(z + tot).astype(BF16), key_tile(v_ref, j))
        carry = jnp.broadcast_to(tot[:, 0:1], (HEADS_PER_BLOCK * m, LANES))
        for h in range(HEADS_PER_BLOCK):
            acc_ref[h, r0:r1, :] += pv[h * m:(h + 1) * m]
            carry_ref[h, r0:r1, :] = carry[h * m:(h + 1) * m]
        return peak(carry)

    peaks = [peak(c) for c in carries]

    def peak_from(i):
        p = peaks[i]
        for other in peaks[i + 1:]:
            p = jnp.maximum(p, other)
        return p

    has_left = qi > 0

    def near_left():
        p = None
        for j in range(-1, -nadj - 1, -1):
            pj = tile(0, (j + nadj + 1) * tk, j)
            p = pj if p is None else jnp.maximum(p, pj)
        return p

    def no_left():
        p = peaks[0]
        for other in peaks[1:nadj]:
            p = jnp.maximum(p, other)
        return p

    near_peak = lax.cond(has_left, near_left, no_left)

    def far_tile(first, j):
        tile(first * tk, tq, j)

    for j in range(nb - 2 - nadj, -nadj - 1, -1):
        first = j + nadj + 1
        need = peak_from(first) > CARRY_EXIT
        pl.when(need if j >= 0 else has_left & need)(functools.partial(far_tile, first, j))

    n_left = qi * nb - nadj

    def more(state):
        it, alive = state
        return (it < n_left) & alive

    def body(state):
        it, _ = state
        return it + 1, tile(0, tq, -(nadj + 1) - it) > CARRY_EXIT

    alive = jnp.maximum(peak_from(nadj), near_peak) > CARRY_EXIT
    lax.while_loop(more, body, (jnp.int32(0), alive))

    out = acc_ref[0]
    for h in range(1, HEADS_PER_BLOCK):
        out = jnp.where(lane >= h * HEAD_DIM, acc_ref[h], out)
    o_ref[0] = out.astype(o_ref.dtype)


def _mixer_b(q, k, v):
    b, seq, width = q.shape
    tq, tk = TQ_B, TK_B
    idx = jnp.arange(tk)
    tri = jnp.where(idx[:, None] >= idx[None, :], -1.0, 0.0).astype(BF16)
    negl = tri
    kern = functools.partial(_mixer_b_kernel, tq=tq, tk=tk)
    return pl.pallas_call(
        kern,
        grid=(b, width // LANES, seq // tq),
        in_specs=[
            pl.BlockSpec((1, tq, LANES), lambda bi, hp, qi: (bi, qi, hp)),
            pl.BlockSpec((1, seq, LANES), lambda bi, hp, qi: (bi, 0, hp)),
            pl.BlockSpec((1, seq, LANES), lambda bi, hp, qi: (bi, 0, hp)),
            pl.BlockSpec((tk, tk), lambda bi, hp, qi: (0, 0)),
        ],
        out_specs=pl.BlockSpec((1, tq, LANES), lambda bi, hp, qi: (bi, qi, hp)),
        out_shape=jax.ShapeDtypeStruct((b, seq, width), BF16),
        scratch_shapes=[pltpu.VMEM((HEADS_PER_BLOCK, tq, LANES), F32),
                        pltpu.VMEM((HEADS_PER_BLOCK, tq, LANES), F32)],
        compiler_params=pltpu.CompilerParams(
            dimension_semantics=("parallel", "parallel", "arbitrary"),
            vmem_limit_bytes=VMEM_LIMIT),
        name="mixer_b",
    )(q, k, v, negl)


def _merge_kernel(x_ref, a_ref, b_ref, ga_ref, gb_ref, wa_ref, wb_ref, wo_ref, g_ref, beta_ref,
                  o_ref):
    slab = x_ref.shape[0] // MERGE_SLABS
    for s in range(MERGE_SLABS):
        rows = slice(s * slab, (s + 1) * slab)
        ya = _dot(a_ref[rows, :], wa_ref[...])
        yb = _dot(b_ref[rows, :], wb_ref[...])
        m = ga_ref[rows, :].astype(F32) * ya + gb_ref[rows, :].astype(F32) * yb
        mix = _dot(m.astype(BF16), wo_ref[...])
        r = DEEPNORM_ALPHA * x_ref[rows, :] + mix
        o_ref[rows, :] = _layer_norm(r, g_ref[...], beta_ref[...])


def _merge(x, a, bm, ga, gb, wa, wb, wo, layer, g, beta):
    n = x.shape[0]
    tm = TM_MERGE
    row = lambda wd: pl.BlockSpec((tm, wd), lambda i: (i, 0))
    full = lambda arr: pl.BlockSpec(arr.shape, lambda i: (0, 0))
    return pl.pallas_call(
        _merge_kernel,
        grid=(n // tm,),
        in_specs=[row(D_MODEL), row(WIDTH_A), row(WIDTH_B), row(D_MODEL), row(D_MODEL),
                  _layer_weight(wa, layer), _layer_weight(wb, layer), _layer_weight(wo, layer),
                  full(g), full(beta)],
        out_specs=row(D_MODEL),
        out_shape=jax.ShapeDtypeStruct((n, D_MODEL), F32),
        compiler_params=pltpu.CompilerParams(
            dimension_semantics=("parallel",), vmem_limit_bytes=VMEM_LIMIT),
        name="merge_ln",
    )(x, a, bm, ga, gb, wa, wb, wo, g, beta)


def _ffn_kernel(x_ref, w1_ref, w2_ref, g_ref, beta_ref, o_ref):
    slab = x_ref.shape[0] // FFN_SLABS
    for s in range(FFN_SLABS):
        rows = slice(s * slab, (s + 1) * slab)
        x = x_ref[rows, :]
        xb = x.astype(BF16)
        gate = _dot(xb, w1_ref[:, :D_FF])
        up = _dot(xb, w1_ref[:, D_FF:])
        act = gate * _sigmoid(gate) * up
        ffn = _dot(act.astype(BF16), w2_ref[...])
        r = DEEPNORM_ALPHA * x + ffn
        o_ref[rows, :] = _layer_norm(r, g_ref[...], beta_ref[...])


def _ffn(x, w1, w2, layer, g, beta):
    n = x.shape[0]
    tm = TM_FFN
    row = pl.BlockSpec((tm, D_MODEL), lambda i: (i, 0))
    full = lambda arr: pl.BlockSpec(arr.shape, lambda i: (0, 0))
    return pl.pallas_call(
        _ffn_kernel,
        grid=(n // tm,),
        in_specs=[row, _layer_weight(w1, layer), _layer_weight(w2, layer), full(g), full(beta)],
        out_specs=row,
        out_shape=jax.ShapeDtypeStruct((n, D_MODEL), F32),
        compiler_params=pltpu.CompilerParams(
            dimension_semantics=("parallel",), vmem_limit_bytes=VMEM_LIMIT),
        name="ffn_ln",
    )(x, w1, w2, g, beta)


def kernel(x, w_in, b_gate, rel_bias, w_proj_a, w_proj_b, w_out, ln1_g, ln1_b, w_ffn_in,
           w_ffn_out, ln2_g, ln2_b):
    b, seq, d = x.shape
    n = b * seq
    depth = w_in.shape[0]
    h = x.reshape(n, d)
    w_in, w_proj_a, w_proj_b, w_out, w_ffn_in, w_ffn_out = (
        w.astype(BF16) for w in (w_in, w_proj_a, w_proj_b, w_out, w_ffn_in, w_ffn_out))
    for l in range(depth):
        qa, ka, va, qb, kb, vb, ga, gb = _inproj(h, w_in, l, b_gate[l].reshape(1, -1))
        shp = (b, seq, -1)
        att_a = _mixer_a(qa.reshape(shp), ka.reshape(shp), va.reshape(shp),
                         _extended_bias(rel_bias[l], TQ_A))
        att_b = _mixer_b(qb.reshape(shp), kb.reshape(shp), vb.reshape(shp))
        h = _merge(h, att_a.reshape(n, -1), att_b.reshape(n, -1), ga, gb,
                   w_proj_a, w_proj_b, w_out, l,
                   ln1_g[l].reshape(1, -1), ln1_b[l].reshape(1, -1))
        h = _ffn(h, w_ffn_in, w_ffn_out, l, ln2_g[l].reshape(1, -1), ln2_b[l].reshape(1, -1))
    return h.reshape(b, seq, d)
```

```python
import functools
import math

import jax
import jax.numpy as jnp
from jax import lax
from jax.experimental import pallas as pl
from jax.experimental.pallas import tpu as pltpu

D_MODEL = 1024
HEAD_DIM = 64
WIDTH_A = 512
WIDTH_B = 512
CHUNK = 64
LEFT_CHUNKS = 8
REL_CLIP = 256
D_FF = 2816
DEPTH = 2
DEEPNORM_ALPHA = (2 * DEPTH) ** 0.25
LN_EPS = 1e-5
LOG2E = 1.4426950408889634
Q_SCALE = LOG2E / math.sqrt(HEAD_DIM)
NEG_BIG = -1e30

LANES = 128
SUBLANES = 8
HEADS_PER_BLOCK = LANES // HEAD_DIM
VMEM_LIMIT = 56 * 1024 * 1024

TM_PROJ = 1024
TM_MERGE = 1024
MERGE_SLABS = 4
TM_FFN = 1024
FFN_SLABS = 4
TQ_A = 128
NSUB_A = 32
KB_A = 128
TQ_B = 2048
TK_B = 256
NEAR_KEYS_B = 256
CARRY_EXIT = -200.0

BF16 = jnp.bfloat16
F32 = jnp.float32


def _dot(a, b):
    return jnp.dot(a, b, preferred_element_type=F32)


def _layer_weight(w, layer):
    return pl.BlockSpec((None,) + w.shape[1:], lambda i: (layer, 0, 0),
                        pipeline_mode=pl.Buffered(1))


def _dot_nt(a, b):
    return lax.dot_general(a, b, (((1,), (1,)), ((), ())), preferred_element_type=F32)


def _layer_norm(r, g, b):
    mu = jnp.mean(r, axis=-1, keepdims=True)
    d = r - mu
    var = jnp.mean(d * d, axis=-1, keepdims=True)
    return d * lax.rsqrt(var + LN_EPS) * g + b


def _sigmoid(v):
    return 1.0 / (1.0 + jnp.exp(-v))


def _inproj_kernel(x_ref, w_ref, bg_ref, qa_ref, ka_ref, va_ref, qb_ref, kb_ref, vb_ref,
                   ga_ref, gb_ref):
    xb = x_ref[...].astype(BF16)

    def proj(c0, width):
        return _dot(xb, w_ref[:, c0:c0 + width])

    c = 3 * WIDTH_A + 3 * WIDTH_B
    ga_ref[...] = _sigmoid(proj(c, D_MODEL) + bg_ref[:, :D_MODEL]).astype(BF16)
    c += D_MODEL
    gb_ref[...] = _sigmoid(proj(c, D_MODEL) + bg_ref[:, D_MODEL:]).astype(BF16)
    c = 0
    qa_ref[...] = (proj(c, WIDTH_A) * Q_SCALE).astype(BF16)
    c += WIDTH_A
    ka_ref[...] = proj(c, WIDTH_A).astype(BF16)
    c += WIDTH_A
    va_ref[...] = proj(c, WIDTH_A).astype(BF16)
    c += WIDTH_A
    qb_ref[...] = (proj(c, WIDTH_B) * Q_SCALE).astype(BF16)
    c += WIDTH_B
    kb_ref[...] = proj(c, WIDTH_B).astype(BF16)
    c += WIDTH_B
    vb_ref[...] = proj(c, WIDTH_B).astype(BF16)


def _inproj(x, w, layer, bg):
    n = x.shape[0]
    widths = [WIDTH_A] * 3 + [WIDTH_B] * 3 + [D_MODEL] * 2
    return pl.pallas_call(
        _inproj_kernel,
        grid=(n // TM_PROJ,),
        in_specs=[
            pl.BlockSpec((TM_PROJ, D_MODEL), lambda i: (i, 0)),
            _layer_weight(w, layer),
            pl.BlockSpec((1, 2 * D_MODEL), lambda i: (0, 0)),
        ],
        out_specs=[pl.BlockSpec((TM_PROJ, wd), lambda i: (i, 0)) for wd in widths],
        out_shape=[jax.ShapeDtypeStruct((n, wd), BF16) for wd in widths],
        compiler_params=pltpu.CompilerParams(
            dimension_semantics=("parallel",), vmem_limit_bytes=VMEM_LIMIT),
        name="inproj",
    )(x, w, bg)


def _mixer_a_kernel(q_ref, k_ref, v_ref, eb_ref, o_ref, bias_ref, *, tq, nsub):
    left = LEFT_CHUNKS * CHUNK
    win = left + tq
    hp = pl.program_id(1)
    qi = pl.program_id(2)

    @pl.when(qi == 0)
    def _build_bias():
        qc = lax.broadcasted_iota(jnp.int32, (tq, win), 0) // CHUNK
        kc = lax.broadcasted_iota(jnp.int32, (tq, win), 1) // CHUNK
        band = (kc >= qc) & (kc <= qc + LEFT_CHUNKS)
        for h in range(HEADS_PER_BLOCK):
            e = eb_ref[pl.ds(hp * HEADS_PER_BLOCK + h, 1), :]
            rows = jnp.broadcast_to(e, (tq, e.shape[1]))
            toep = pltpu.roll(rows, 0, 1, stride=1, stride_axis=0)[:, :win]
            bias_ref[h] = jnp.where(band, toep * LOG2E, NEG_BIG)

    assert HEADS_PER_BLOCK == 2 and tq == KB_A and tq * nsub >= left
    wb = left // KB_A + 1
    q = q_ref[0]
    lane = lax.broadcasted_iota(jnp.int32, q.shape, 1)
    vlane = lax.broadcasted_iota(jnp.int32, (KB_A, LANES), 1)
    first = lane < HEAD_DIM
    qms = [jnp.where(first, q, jnp.zeros_like(q)), jnp.where(first, jnp.zeros_like(q), q)]

    def step(first_block):
        t0 = qi * (nsub * tq)
        blocks = range(first_block, nsub + wb - 1)
        users = {b: range(max(0, b - wb + 1), min(nsub - 1, b) + 1) for b in blocks}
        starts = {b: pl.multiple_of(t0 - left + b * KB_A, KB_A) for b in blocks}

        s = {}
        for b in blocks:
            r0, r1 = users[b][0] * tq, (users[b][-1] + 1) * tq
            lhs = jnp.concatenate([qms[0][r0:r1], qms[1][r0:r1]], axis=0)
            sb = _dot_nt(lhs, k_ref[0, pl.ds(starts[b], KB_A), :])
            for h in range(HEADS_PER_BLOCK):
                for i in users[b]:
                    off = h * (r1 - r0) + (i - users[b][0]) * tq
                    kb = b - i
                    s[h, i, b] = sb[off:off + tq] + bias_ref[h, :, kb * KB_A:(kb + 1) * KB_A]

        p = {}
        for h in range(HEADS_PER_BLOCK):
            for i in range(nsub):
                mine = [s[h, i, b] for b in blocks if i in users[b]]
                m = mine[0]
                for sb in mine[1:]:
                    m = jnp.maximum(m, sb)
                for c in range(1, KB_A // LANES):
                    m = jnp.maximum(m[:, :LANES], m[:, c * LANES:(c + 1) * LANES])
                m = m.max(axis=-1, keepdims=True)
                for b in blocks:
                    if i in users[b]:
                        p[h, i, b] = jnp.exp2(s[h, i, b] - m).astype(BF16)

        acc = {}
        for b in blocks:
            vblk = v_ref[0, pl.ds(starts[b], KB_A), :]
            for h in range(HEADS_PER_BLOCK):
                mine = (vlane < HEAD_DIM) if h == 0 else (vlane >= HEAD_DIM)
                vaug = jnp.where(mine, vblk, jnp.ones_like(vblk))
                pv = _dot(jnp.concatenate([p[h, i, b] for i in users[b]], axis=0), vaug)
                for n, i in enumerate(users[b]):
                    part = pv[n * tq:(n + 1) * tq]
                    acc[h, i] = part if (h, i) not in acc else acc[h, i] + part

        fl = lax.broadcasted_iota(jnp.int32, (tq, LANES), 1) < HEAD_DIM
        for i in range(nsub):
            num = jnp.where(fl, acc[0, i], acc[1, i])
            den = pltpu.roll(jnp.where(fl, acc[1, i], acc[0, i]), HEAD_DIM, 1)
            o_ref[0, i * tq:(i + 1) * tq, :] = (num / den).astype(o_ref.dtype)

    pl.when(qi == 0)(functools.partial(step, left // KB_A))
    pl.when(qi != 0)(functools.partial(step, 0))


def _mixer_a(q, k, v, ebias):
    b, seq, width = q.shape
    tq, nsub = TQ_A, NSUB_A
    win = LEFT_CHUNKS * CHUNK + tq
    kern = functools.partial(_mixer_a_kernel, tq=tq, nsub=nsub)
    return pl.pallas_call(
        kern,
        grid=(b, width // LANES, seq // (tq * nsub)),
        in_specs=[
            pl.BlockSpec((1, tq * nsub, LANES), lambda bi, hp, qi: (bi, qi, hp)),
            pl.BlockSpec((1, seq, LANES), lambda bi, hp, qi: (bi, 0, hp)),
            pl.BlockSpec((1, seq, LANES), lambda bi, hp, qi: (bi, 0, hp)),
            pl.BlockSpec(ebias.shape, lambda bi, hp, qi: (0, 0)),
        ],
        out_specs=pl.BlockSpec((1, tq * nsub, LANES), lambda bi, hp, qi: (bi, qi, hp)),
        out_shape=jax.ShapeDtypeStruct((b, seq, width), BF16),
        scratch_shapes=[pltpu.VMEM((HEADS_PER_BLOCK, tq, win), F32)],
        compiler_params=pltpu.CompilerParams(
            dimension_semantics=("parallel", "parallel", "arbitrary"),
            vmem_limit_bytes=VMEM_LIMIT),
        name="mixer_a",
    )(q, k, v, ebias)


def _extended_bias(rel_bias, tq):
    left = LEFT_CHUNKS * CHUNK
    top = rel_bias[:, 2 * REL_CLIP:]
    n_flat = left - REL_CLIP + 1
    n_rev = left + tq - n_flat
    lo = 2 * REL_CLIP - n_rev
    rev = rel_bias[:, lo:2 * REL_CLIP][:, ::-1]
    h = rel_bias.shape[0]
    return jnp.concatenate(
        [jnp.broadcast_to(top, (h, n_flat)), rev, jnp.broadcast_to(top, (h, tq))], axis=1)


EXP2_CLAMP = 126.0


def _softplus2(z):
    return jnp.maximum(z, jnp.log(1.0 + jnp.exp2(jnp.minimum(z, EXP2_CLAMP))) * LOG2E)


def _mixer_b_kernel(q_ref, k_ref, v_ref, negl_ref, o_ref, acc_ref, carry_ref, *, tq, tk):
    qi = pl.program_id(2)
    t0 = pl.multiple_of(qi * tq, tq)
    q = q_ref[0]
    lane = lax.broadcasted_iota(jnp.int32, q.shape, 1)
    qms = []
    for h in range(HEADS_PER_BLOCK):
        in_head = (lane >= h * HEAD_DIM) & (lane < (h + 1) * HEAD_DIM)
        qms.append(jnp.where(in_head, q, jnp.zeros_like(q)))
    negl = negl_ref[...]
    reps = tk // LANES
    nb = tq // tk
    nadj = NEAR_KEYS_B // tk
    hb = HEADS_PER_BLOCK * tk

    def lane_tile(c):
        return jnp.concatenate([c] * reps, axis=1) if reps > 1 else c

    def key_tile(ref, j):
        return ref[0, pl.ds(pl.multiple_of(t0 + j * tk, tk), tk), :]

    row = lax.broadcasted_iota(jnp.int32, (hb, tk), 0)
    col = lax.broadcasted_iota(jnp.int32, (hb, tk), 1)
    causal = col < (row & (tk - 1))

    def diag_masked(x):
        head = jnp.where(causal, x[:hb], 0.0)
        return head if x.shape[0] == hb else jnp.concatenate([head, x[hb:]], axis=0)

    zs, negcs = [], []
    for j in range(nb):
        blocks = range(j, min(j + nadj, nb - 1) + 1)
        lhs = jnp.concatenate([qm[i * tk:(i + 1) * tk] for i in blocks for qm in qms], axis=0)
        z = _dot_nt(lhs, key_tile(k_ref, j))
        zs.append(z)
        negcs.append(_dot(diag_masked(_softplus2(z)).astype(BF16), negl))
    tots = [[None] * (min(j + nadj, nb - 1) - j + 1) for j in range(nb)]
    carries = []
    for i in range(nb):
        carry = None
        for j in range(i, max(i - nadj, 0) - 1, -1):
            tot = negcs[j][(i - j) * hb:(i - j + 1) * hb]
            if carry is not None:
                tot = tot + lane_tile(carry)
            tots[j][i - j] = tot
            carry = jnp.broadcast_to(tot[:, 0:1], (hb, LANES))
        carries.append(carry)
    pvs = []
    for j in range(nb):
        w = diag_masked(jnp.exp2(zs[j] + jnp.concatenate(tots[j], axis=0)))
        pvs.append(_dot(w.astype(BF16), key_tile(v_ref, j)))
    for i in range(nb):
        acc = None
        for j in range(i, max(i - nadj, 0) - 1, -1):
            part = pvs[j][(i - j) * hb:(i - j + 1) * hb]
            acc = part if acc is None else acc + part
        for h in range(HEADS_PER_BLOCK):
            acc_ref[h, i * tk:(i + 1) * tk, :] = acc[h * tk:(h + 1) * tk]
            carry_ref[h, i * tk:(i + 1) * tk, :] = carries[i][h * tk:(h + 1) * tk]

    def peak(c):
        return jnp.max(c.reshape(c.shape[0] // SUBLANES, SUBLANES, LANES).max(axis=0))

    def tile(r0, r1, j):
        m = r1 - r0
        z = _dot_nt(jnp.concatenate([qm[r0:r1] for qm in qms], axis=0), key_tile(k_ref, j))
        negc = _dot(_softplus2(z).astype(BF16), negl)
        carry = jnp.concatenate([carry_ref[h, r0:r1, :] for h in range(HEADS_PER_BLOCK)], axis=0)
        tot = negc + lane_tile(carry)
        pv = _dot(jnp.exp2(z + tot).astype(BF16), key_tile(v_ref, j))
        carry = jnp.broadcast_to(tot[:, 0:1], (HEADS_PER_BLOCK * m, LANES))
        for h in range(HEADS_PER_BLOCK):
            acc_ref[h, r0:r1, :] += pv[h * m:(h + 1) * m]
            carry_ref[h, r0:r1, :] = carry[h * m:(h + 1) * m]
        return peak(carry)

    peaks = [peak(c) for c in carries]

    def peak_from(i):
        p = peaks[i]
        for other in peaks[i + 1:]:
            p = jnp.maximum(p, other)
        return p

    has_left = qi > 0

    def near_left():
        p = None
        for j in range(-1, -nadj - 1, -1):
            pj = tile(0, (j + nadj + 1) * tk, j)
            p = pj if p is None else jnp.maximum(p, pj)
        return p

    def no_left():
        p = peaks[0]
        for other in peaks[1:nadj]:
            p = jnp.maximum(p, other)
        return p

    near_peak = lax.cond(has_left, near_left, no_left)

    def far_tile(first, j):
        tile(first * tk, tq, j)

    for j in range(nb - 2 - nadj, -nadj - 1, -1):
        first = j + nadj + 1
        need = peak_from(first) > CARRY_EXIT
        pl.when(need if j >= 0 else has_left & need)(functools.partial(far_tile, first, j))

    n_left = qi * nb - nadj

    def more(state):
        it, alive = state
        return (it < n_left) & alive

    def body(state):
        it, _ = state
        return it + 1, tile(0, tq, -(nadj + 1) - it) > CARRY_EXIT

    alive = jnp.maximum(peak_from(nadj), near_peak) > CARRY_EXIT
    lax.while_loop(more, body, (jnp.int32(0), alive))

    out = acc_ref[0]
    for h in range(1, HEADS_PER_BLOCK):
        out = jnp.where(lane >= h * HEAD_DIM, acc_ref[h], out)
    o_ref[0] = out.astype(o_ref.dtype)


def _mixer_b(q, k, v):
    b, seq, width = q.shape
    tq, tk = TQ_B, TK_B
    idx = jnp.arange(tk)
    tri = jnp.where(idx[:, None] >= idx[None, :], -1.0, 0.0).astype(BF16)
    negl = tri
    kern = functools.partial(_mixer_b_kernel, tq=tq, tk=tk)
    return pl.pallas_call(
        kern,
        grid=(b, width // LANES, seq // tq),
        in_specs=[
            pl.BlockSpec((1, tq, LANES), lambda bi, hp, qi: (bi, qi, hp)),
            pl.BlockSpec((1, seq, LANES), lambda bi, hp, qi: (bi, 0, hp)),
            pl.BlockSpec((1, seq, LANES), lambda bi, hp, qi: (bi, 0, hp)),
            pl.BlockSpec((tk, tk), lambda bi, hp, qi: (0, 0)),
        ],
        out_specs=pl.BlockSpec((1, tq, LANES), lambda bi, hp, qi: (bi, qi, hp)),
        out_shape=jax.ShapeDtypeStruct((b, seq, width), BF16),
        scratch_shapes=[pltpu.VMEM((HEADS_PER_BLOCK, tq, LANES), F32),
                        pltpu.VMEM((HEADS_PER_BLOCK, tq, LANES), F32)],
        compiler_params=pltpu.CompilerParams(
            dimension_semantics=("parallel", "parallel", "arbitrary"),
            vmem_limit_bytes=VMEM_LIMIT),
        name="mixer_b",
    )(q, k, v, negl)


def _mixers_kernel(qa_ref, ka_ref, va_ref, eb_ref, qb_ref, kb_ref, vb_ref, negl_ref,
                   oa_ref, ob_ref, bias_ref, acc_ref, carry_ref, *, tq_a, nsub, tq_b, tk):
    _mixer_a_kernel(qa_ref, ka_ref, va_ref, eb_ref, oa_ref, bias_ref, tq=tq_a, nsub=nsub)
    _mixer_b_kernel(qb_ref, kb_ref, vb_ref, negl_ref, ob_ref, acc_ref, carry_ref, tq=tq_b, tk=tk)


def _mixers(qa, ka, va, ebias, qb, kb, vb):
    b, seq, width = qa.shape
    tq, tk = TQ_B, TK_B
    nsub = tq // TQ_A
    win = LEFT_CHUNKS * CHUNK + TQ_A
    idx = jnp.arange(tk)
    negl = jnp.where(idx[:, None] >= idx[None, :], -1.0, 0.0).astype(BF16)
    kern = functools.partial(_mixers_kernel, tq_a=TQ_A, nsub=nsub, tq_b=tq, tk=tk)
    q_spec = pl.BlockSpec((1, tq, LANES), lambda bi, hp, qi: (bi, qi, hp))
    kv_spec = pl.BlockSpec((1, seq, LANES), lambda bi, hp, qi: (bi, 0, hp))
    return pl.pallas_call(
        kern,
        grid=(b, width // LANES, seq // tq),
        in_specs=[q_spec, kv_spec, kv_spec,
                  pl.BlockSpec(ebias.shape, lambda bi, hp, qi: (0, 0)),
                  q_spec, kv_spec, kv_spec,
                  pl.BlockSpec((tk, tk), lambda bi, hp, qi: (0, 0))],
        out_specs=[q_spec, q_spec],
        out_shape=[jax.ShapeDtypeStruct((b, seq, width), BF16)] * 2,
        scratch_shapes=[pltpu.VMEM((HEADS_PER_BLOCK, TQ_A, win), F32),
                        pltpu.VMEM((HEADS_PER_BLOCK, tq, LANES), F32),
                        pltpu.VMEM((HEADS_PER_BLOCK, tq, LANES), F32)],
        compiler_params=pltpu.CompilerParams(
            dimension_semantics=("parallel", "parallel", "arbitrary"),
            vmem_limit_bytes=VMEM_LIMIT),
        name="mixers",
    )(qa, ka, va, ebias, qb, kb, vb, negl)


def _merge_kernel(x_ref, a_ref, b_ref, ga_ref, gb_ref, wa_ref, wb_ref, wo_ref, g_ref, beta_ref,
                  o_ref):
    slab = x_ref.shape[0] // MERGE_SLABS
    for s in range(MERGE_SLABS):
        rows = slice(s * slab, (s + 1) * slab)
        ya = _dot(a_ref[rows, :], wa_ref[...])
        yb = _dot(b_ref[rows, :], wb_ref[...])
        m = ga_ref[rows, :].astype(F32) * ya + gb_ref[rows, :].astype(F32) * yb
        mix = _dot(m.astype(BF16), wo_ref[...])
        r = DEEPNORM_ALPHA * x_ref[rows, :] + mix
        o_ref[rows, :] = _layer_norm(r, g_ref[...], beta_ref[...])


def _merge(x, a, bm, ga, gb, wa, wb, wo, layer, g, beta):
    n = x.shape[0]
    tm = TM_MERGE
    row = lambda wd: pl.BlockSpec((tm, wd), lambda i: (i, 0))
    full = lambda arr: pl.BlockSpec(arr.shape, lambda i: (0, 0))
    return pl.pallas_call(
        _merge_kernel,
        grid=(n // tm,),
        in_specs=[row(D_MODEL), row(WIDTH_A), row(WIDTH_B), row(D_MODEL), row(D_MODEL),
                  _layer_weight(wa, layer), _layer_weight(wb, layer), _layer_weight(wo, layer),
                  full(g), full(beta)],
        out_specs=row(D_MODEL),
        out_shape=jax.ShapeDtypeStruct((n, D_MODEL), F32),
        compiler_params=pltpu.CompilerParams(
            dimension_semantics=("parallel",), vmem_limit_bytes=VMEM_LIMIT),
        name="merge_ln",
    )(x, a, bm, ga, gb, wa, wb, wo, g, beta)


def _ffn_kernel(x_ref, w1_ref, w2_ref, g_ref, beta_ref, o_ref):
    slab = x_ref.shape[0] // FFN_SLABS
    for s in range(FFN_SLABS):
        rows = slice(s * slab, (s + 1) * slab)
        x = x_ref[rows, :]
        xb = x.astype(BF16)
        gate = _dot(xb, w1_ref[:, :D_FF])
        up = _dot(xb, w1_ref[:, D_FF:])
        act = gate * _sigmoid(gate) * up
        ffn = _dot(act.astype(BF16), w2_ref[...])
        r = DEEPNORM_ALPHA * x + ffn
        o_ref[rows, :] = _layer_norm(r, g_ref[...], beta_ref[...])


def _ffn(x, w1, w2, layer, g, beta):
    n = x.shape[0]
    tm = TM_FFN
    row = pl.BlockSpec((tm, D_MODEL), lambda i: (i, 0))
    full = lambda arr: pl.BlockSpec(arr.shape, lambda i: (0, 0))
    return pl.pallas_call(
        _ffn_kernel,
        grid=(n // tm,),
        in_specs=[row, _layer_weight(w1, layer), _layer_weight(w2, layer), full(g), full(beta)],
        out_specs=row,
        out_shape=jax.ShapeDtypeStruct((n, D_MODEL), F32),
        compiler_params=pltpu.CompilerParams(
            dimension_semantics=("parallel",), vmem_limit_bytes=VMEM_LIMIT),
        name="ffn_ln",
    )(x, w1, w2, g, beta)


def kernel(x, w_in, b_gate, rel_bias, w_proj_a, w_proj_b, w_out, ln1_g, ln1_b, w_ffn_in,
           w_ffn_out, ln2_g, ln2_b):
    b, seq, d = x.shape
    n = b * seq
    depth = w_in.shape[0]
    h = x.reshape(n, d)
    w_in, w_proj_a, w_proj_b, w_out, w_ffn_in, w_ffn_out = (
        w.astype(BF16) for w in (w_in, w_proj_a, w_proj_b, w_out, w_ffn_in, w_ffn_out))
    for l in range(depth):
        qa, ka, va, qb, kb, vb, ga, gb = _inproj(h, w_in, l, b_gate[l].reshape(1, -1))
        shp = (b, seq, -1)
        att_a, att_b = _mixers(qa.reshape(shp), ka.reshape(shp), va.reshape(shp),
                               _extended_bias(rel_bias[l], TQ_A),
                               qb.reshape(shp), kb.reshape(shp), vb.reshape(shp))
        h = _merge(h, att_a.reshape(n, -1), att_b.reshape(n, -1), ga, gb,
                   w_proj_a, w_proj_b, w_out, l,
                   ln1_g[l].reshape(1, -1), ln1_b[l].reshape(1, -1))
        h = _ffn(h, w_ffn_in, w_ffn_out, l, ln2_g[l].reshape(1, -1), ln2_b[l].reshape(1, -1))
    return h.reshape(b, seq, d)
```
